```python
import jax, jax.numpy as jnp
from jax import lax
import numpy as np

D_MODEL = 1024
BATCH = 16
SEQ = 2048
DEPTH = 1

PLE_DIM = 256
ATTN_WIDTH = D_MODEL // 2
ATTN_HEAD_DIM = 64
N_ATTN_HEADS = ATTN_WIDTH // ATTN_HEAD_DIM
ROPE_THETA = 10000.0
MOBA_BLOCK = 256
MOBA_TOPK = 3
MOBA_QUERY_CHUNK = 16
GMLP_WIDTH = D_MODEL - ATTN_WIDTH
GMLP_GROUPS = 4
GMLP_GROUP_DIM = GMLP_WIDTH // GMLP_GROUPS
GMLP_CHUNK = 128
MIX_WIDTH = ATTN_WIDTH + GMLP_WIDTH
IN_PROJ_WIDTH = 3 * ATTN_WIDTH + 2 * GMLP_WIDTH
N_EXPERTS = 32
TOP_K = 4
D_EXPERT = D_MODEL
SWIGLU_LIMIT = 7.0
SWIGLU_ALPHA = 1.702
EXPERT_ROW_BLOCK = 512
NORM_EPS = 1e-6
NEG_INF = -1e30

kernel_name = "hymba_moba_gmlp_moe_ple"


def rmsnorm(x, g):
    xf = x.astype(jnp.float32)
    y = xf * lax.rsqrt(jnp.mean(xf * xf, axis=-1, keepdims=True) + NORM_EPS)
    return (y * g.astype(jnp.float32)).astype(x.dtype)


def layernorm(x, g, b):
    xf = x.astype(jnp.float32)
    mu = jnp.mean(xf, axis=-1, keepdims=True)
    var = jnp.mean(jnp.square(xf - mu), axis=-1, keepdims=True)
    y = (xf - mu) * lax.rsqrt(var + NORM_EPS)
    return (y * g.astype(jnp.float32) + b.astype(jnp.float32)).astype(x.dtype)


def rope(x):
    S, hd = x.shape[2], x.shape[3]
    half = hd // 2
    inv = ROPE_THETA ** (-jnp.arange(half, dtype=jnp.float32) / half)
    ang = jnp.arange(S, dtype=jnp.float32)[:, None] * inv[None, :]
    cos = jnp.cos(ang).astype(x.dtype)
    sin = jnp.sin(ang).astype(x.dtype)
    x1, x2 = x[..., :half], x[..., half:]
    return jnp.concatenate([x1 * cos - x2 * sin, x2 * cos + x1 * sin], axis=-1)


def moba_attention(q, k, v):
    B, H, S, hd = q.shape
    n_blk = -(-S // MOBA_BLOCK)
    S_pad = n_blk * MOBA_BLOCK
    pad = ((0, 0), (0, 0), (0, S_pad - S), (0, 0))
    q, k, v = jnp.pad(q, pad), jnp.pad(k, pad), jnp.pad(v, pad)
    scale = 1.0 / np.sqrt(hd)
    k_blocks = k.reshape(B, H, n_blk, MOBA_BLOCK, hd)
    v_blocks = v.reshape(B, H, n_blk, MOBA_BLOCK, hd)
    k_mean = jnp.mean(k_blocks.astype(jnp.float32), axis=3)

    q_block = jnp.arange(S_pad) // MOBA_BLOCK
    gate = jnp.einsum('bhsd,bhnd->bhsn', q.astype(jnp.float32), k_mean)
    fully_past = jnp.arange(n_blk)[None, :] < q_block[:, None]
    gate = jnp.where(fully_past, gate, NEG_INF)
    k_sel = min(MOBA_TOPK, n_blk)
    _, sel_idx = lax.top_k(gate, k_sel)
    sel_valid = jnp.arange(k_sel)[None, :] < q_block[:, None]

    QC = MOBA_QUERY_CHUNK
    n_qc = S_pad // QC
    q_c = q.reshape(B, H, n_qc, QC, hd).transpose(2, 0, 1, 3, 4)
    idx_c = sel_idx.reshape(B, H, n_qc, QC, k_sel).transpose(2, 0, 1, 3, 4)
    valid_c = sel_valid.reshape(n_qc, QC, k_sel)
    b_ix = jnp.arange(B)[:, None, None, None]
    h_ix = jnp.arange(H)[None, :, None, None]

    def query_chunk(args):
        c, qq, idx, valid = args
        q_pos = c * QC + jnp.arange(QC)
        own = (c * QC) // MOBA_BLOCK
        k_own = lax.dynamic_index_in_dim(k_blocks, own, axis=2, keepdims=False)
        v_own = lax.dynamic_index_in_dim(v_blocks, own, axis=2, keepdims=False)
        k_pos = own * MOBA_BLOCK + jnp.arange(MOBA_BLOCK)
        s_own = jnp.einsum('bhqd,bhkd->bhqk', qq, k_own).astype(jnp.float32) * scale
        s_own = jnp.where(k_pos[None, :] <= q_pos[:, None], s_own, NEG_INF)
        k_g = k_blocks[b_ix, h_ix, idx]
        v_g = v_blocks[b_ix, h_ix, idx]
        s_past = jnp.einsum('bhqd,bhqrkd->bhqrk', qq, k_g).astype(jnp.float32) * scale
        s_past = jnp.where(valid[None, None, :, :, None], s_past, NEG_INF)
        s = jnp.concatenate([s_past.reshape(B, H, QC, k_sel * MOBA_BLOCK), s_own], axis=-1)
        prob = jax.nn.softmax(s, axis=-1)
        p_past = prob[..., :k_sel * MOBA_BLOCK].reshape(B, H, QC, k_sel, MOBA_BLOCK).astype(v.dtype)
        p_own = prob[..., k_sel * MOBA_BLOCK:].astype(v.dtype)
        return (jnp.einsum('bhqrk,bhqrkd->bhqd', p_past, v_g)
                + jnp.einsum('bhqk,bhkd->bhqd', p_own, v_own))

    out = lax.map(query_chunk, (jnp.arange(n_qc), q_c, idx_c, valid_c))
    out = out.transpose(1, 2, 0, 3, 4).reshape(B, H, S_pad, hd)
    return out[:, :, :S]


def gmlp_spatial_gating(u, v, ln_g, ln_b, w_s, b_s):
    B, S, _ = v.shape
    nc = S // GMLP_CHUNK
    u = jax.nn.gelu(u, approximate=False)
    v = jax.nn.gelu(v, approximate=False).reshape(B, S, GMLP_GROUPS, GMLP_GROUP_DIM)
    v = layernorm(v, ln_g.reshape(GMLP_GROUPS, GMLP_GROUP_DIM), ln_b.reshape(GMLP_GROUPS, GMLP_GROUP_DIM))
    v = v.reshape(B, nc, GMLP_CHUNK, GMLP_GROUPS, GMLP_GROUP_DIM)
    causal = jnp.tril(jnp.ones((GMLP_CHUNK, GMLP_CHUNK), dtype=w_s.dtype))
    w = w_s * causal[None]
    mixed = jnp.einsum('gts,bcsgd->bctgd', w, v) + b_s.T[None, None, :, :, None]
    out = u.reshape(B, nc, GMLP_CHUNK, GMLP_GROUPS, GMLP_GROUP_DIM) * mixed
    return out.reshape(B, S, GMLP_WIDTH)


def routed_experts(h, router_w, router_b, w_gu, b_gu, w_dn, b_dn):
    B, S, D = h.shape
    T = B * S
    hf = h.reshape(T, D)
    logits = (hf @ router_w + router_b).astype(jnp.float32)
    top_v, top_i = lax.top_k(logits, TOP_K)
    gate = jax.nn.softmax(top_v, axis=-1)

    flat_e = top_i.reshape(-1)
    flat_t = jnp.repeat(jnp.arange(T, dtype=jnp.int32), TOP_K)
    flat_g = gate.reshape(-1)
    order = jnp.argsort(flat_e)
    se, st, sg = flat_e[order], flat_t[order], flat_g[order]
    counts = jnp.bincount(flat_e, length=N_EXPERTS)
    padded = (counts + EXPERT_ROW_BLOCK - 1) // EXPERT_ROW_BLOCK * EXPERT_ROW_BLOCK
    start = jnp.cumsum(counts) - counts
    pend = jnp.cumsum(padded)
    pstart = pend - padded
    dest = pstart[se] + (jnp.arange(T * TOP_K) - start[se])
    n_rows = T * TOP_K + N_EXPERTS * EXPERT_ROW_BLOCK
    n_blocks = n_rows // EXPERT_ROW_BLOCK
    row_tok = jnp.full((n_rows,), T, dtype=jnp.int32).at[dest].set(st)
    row_g = jnp.zeros((n_rows,), dtype=jnp.float32).at[dest].set(sg)
    blk_e = jnp.minimum(jnp.searchsorted(pend, jnp.arange(n_blocks) * EXPERT_ROW_BLOCK, side='right'),
                        N_EXPERTS - 1)
    h_pad = jnp.concatenate([hf, jnp.zeros((1, D), hf.dtype)], axis=0)
    x_rows = h_pad[row_tok].reshape(n_blocks, EXPERT_ROW_BLOCK, D)

    def expert_block(args):
        e, xb = args
        gu = xb @ w_gu[e] + b_gu[e]
        g, lin = gu[:, :D_EXPERT], gu[:, D_EXPERT:]
        g = jnp.minimum(g, SWIGLU_LIMIT)
        lin = jnp.clip(lin, -SWIGLU_LIMIT, SWIGLU_LIMIT)
        a = g * jax.nn.sigmoid(SWIGLU_ALPHA * g) * (lin + 1.0)
        return a @ w_dn[e] + b_dn[e]

    y_rows = lax.map(expert_block, (blk_e, x_rows)).reshape(n_rows, D)
    y = jax.ops.segment_sum(y_rows * row_g[:, None].astype(y_rows.dtype), row_tok,
                            num_segments=T + 1)[:T]
    return y.reshape(B, S, D)


def setup_inputs(seed: int = 0) -> dict:
    key = jax.random.key(seed)
    ks = jax.random.split(key, 24)
    f32 = jnp.float32
    nrm = lambda k, shape, s: jax.random.normal(k, shape, f32) * s
    gain = lambda k, shape: 1.0 + 0.02 * jax.random.normal(k, shape, f32)
    L, D, E, F = DEPTH, D_MODEL, N_EXPERTS, D_EXPERT
    return {
        "x": jax.random.normal(ks[0], (BATCH, SEQ, D), f32),
        "p": jax.random.normal(ks[1], (DEPTH, BATCH, SEQ, PLE_DIM), f32),
        "mix_norm": gain(ks[2], (L, D)),
        "w_in": nrm(ks[3], (L, D, IN_PROJ_WIDTH), D ** -0.5),
        "gmlp_ln_g": gain(ks[4], (L, GMLP_WIDTH)),
        "gmlp_ln_b": nrm(ks[5], (L, GMLP_WIDTH), 0.02),
        "gmlp_ws": nrm(ks[6], (L, GMLP_GROUPS, GMLP_CHUNK, GMLP_CHUNK), GMLP_CHUNK ** -0.5),
        "gmlp_bs": gain(ks[7], (L, GMLP_GROUPS, GMLP_CHUNK)),
        "attn_out_norm": gain(ks[8], (L, ATTN_WIDTH)),
        "gmlp_out_norm": gain(ks[9], (L, GMLP_WIDTH)),
        "w_out": nrm(ks[10], (L, MIX_WIDTH, D), MIX_WIDTH ** -0.5),
        "ffn_norm": gain(ks[11], (L, D)),
        "router_w": nrm(ks[12], (L, D, E), D ** -0.5),
        "router_b": nrm(ks[13], (L, E), 0.01),
        "w_gate_up": nrm(ks[14], (L, E, D, 2 * F), D ** -0.5),
        "b_gate_up": nrm(ks[15], (L, E, 2 * F), 0.01),
        "w_down": nrm(ks[16], (L, E, F, D), F ** -0.5),
        "b_down": nrm(ks[17], (L, E, D), 0.01),
        "ple_norm": gain(ks[18], (L, D)),
        "w_ple_gate": nrm(ks[19], (L, D, D), D ** -0.5),
        "w_ple_proj": nrm(ks[20], (L, PLE_DIM, D), PLE_DIM ** -0.5),
        "ple_post_norm": gain(ks[21], (L, D)),
        "final_norm": gain(ks[22], (D,)),
    }


def reference(x, p, mix_norm, w_in, gmlp_ln_g, gmlp_ln_b, gmlp_ws, gmlp_bs, attn_out_norm,
              gmlp_out_norm, w_out, ffn_norm, router_w, router_b, w_gate_up, b_gate_up,
              w_down, b_down, ple_norm, w_ple_gate, w_ple_proj, ple_post_norm, final_norm):
    B, S, D = x.shape
    A = ATTN_WIDTH
    for i in range(DEPTH):
        h = rmsnorm(x, mix_norm[i])
        proj = h @ w_in[i]
        heads = lambda t: t.reshape(B, S, N_ATTN_HEADS, ATTN_HEAD_DIM).transpose(0, 2, 1, 3)
        q = rope(heads(proj[..., 0:A]))
        k = rope(heads(proj[..., A:2 * A]))
        v = heads(proj[..., 2 * A:3 * A])
        u_g = proj[..., 3 * A:3 * A + GMLP_WIDTH]
        v_g = proj[..., 3 * A + GMLP_WIDTH:]
        attn = moba_attention(q, k, v).transpose(0, 2, 1, 3).reshape(B, S, A)
        gm = gmlp_spatial_gating(u_g, v_g, gmlp_ln_g[i], gmlp_ln_b[i], gmlp_ws[i], gmlp_bs[i])
        mixed = jnp.concatenate([rmsnorm(attn, attn_out_norm[i]), rmsnorm(gm, gmlp_out_norm[i])], axis=-1)
        x = x + mixed @ w_out[i]
        x = x + routed_experts(rmsnorm(x, ffn_norm[i]), router_w[i], router_b[i],
                               w_gate_up[i], b_gate_up[i], w_down[i], b_down[i])
        gate = jax.nn.sigmoid(rmsnorm(x, ple_norm[i]) @ w_ple_gate[i])
        x = x + gate * rmsnorm(p[i] @ w_ple_proj[i], ple_post_norm[i])
    return rmsnorm(x, final_norm)
```

```python
import functools

import jax
import jax.numpy as jnp
import numpy as np
from jax import lax
from jax.experimental import pallas as pl
from jax.experimental.pallas import tpu as pltpu

F32 = jnp.float32
BF16 = jnp.bfloat16

D_MODEL = 1024
PLE_DIM = 256
ATTN_WIDTH = 512
HEAD_DIM = 64
ROPE_THETA = 10000.0
MOBA_BLOCK = 256
MOBA_TOPK = 3
GMLP_WIDTH = 512
GMLP_GROUPS = 4
GMLP_GROUP_DIM = 128
GMLP_CHUNK = 128
N_EXPERTS = 32
TOP_K = 4
D_EXPERT = 1024
SWIGLU_LIMIT = 7.0
SWIGLU_ALPHA = 1.702
NORM_EPS = 1e-6
NEG_INF = -1e30

LANES = 128
SUBLANES = 8
ROW_SLABS = D_MODEL // LANES

TM = 512
RB = 512
FC = 512
VMEM_LIMIT = 48 * 1024 * 1024


def _rms(x, g):
    return x * lax.rsqrt(jnp.mean(x * x, axis=-1, keepdims=True) + NORM_EPS) * g


def _nt(a, b):
    return lax.dot_general(a, b, (((1,), (1,)), ((), ())), preferred_element_type=F32)


def _proj_kernel(x_ref, mixn_ref, wqk_ref, wvt_ref, wuv_ref, cos_ref, sin_ref, lng_ref, lnb_ref,
                 ws_ref, bs_ref, gon_ref, q_ref, k_ref, vt_ref, kmean_ref, gm_ref, gacc_ref):
    h = _rms(x_ref[...], mixn_ref[...]).astype(BF16)
    qk = jnp.dot(h, wqk_ref[...], preferred_element_type=F32)
    cos = cos_ref[...]
    sin = sin_ref[...]
    lane = lax.broadcasted_iota(jnp.int32, (TM, LANES), 1)
    first_half = (lane % HEAD_DIM) < (HEAD_DIM // 2)

    def rope(t):
        parts = []
        for s in range(ATTN_WIDTH // LANES):
            slab = t[:, s * LANES:(s + 1) * LANES]
            ahead = pltpu.roll(slab, LANES - HEAD_DIM // 2, 1)
            behind = pltpu.roll(slab, HEAD_DIM // 2, 1)
            parts.append(jnp.where(first_half, ahead, behind))
        return t * cos + jnp.concatenate(parts, axis=1) * sin

    q = rope(qk[:, :ATTN_WIDTH]) * (1.0 / np.sqrt(HEAD_DIM))
    k = rope(qk[:, ATTN_WIDTH:])
    q_ref[...] = q.astype(BF16)
    k_ref[...] = k.astype(BF16)
    for c in range(TM // MOBA_BLOCK):
        kmean_ref[0, c:c + 1, :] = jnp.mean(k[c * MOBA_BLOCK:(c + 1) * MOBA_BLOCK], axis=0, keepdims=True)

    vt = _nt(wvt_ref[...], h).astype(BF16)
    for c in range(TM // MOBA_BLOCK):
        vt_ref[c] = vt[:, c * MOBA_BLOCK:(c + 1) * MOBA_BLOCK]

    uv = jnp.dot(h, wuv_ref[...], preferred_element_type=F32)
    inv_sqrt2 = np.float32(1.0 / np.sqrt(2.0))

    def gelu(t):
        return 0.5 * t * (1.0 + lax.erf(t * inv_sqrt2))

    gu = gelu(uv[:, :GMLP_WIDTH])
    gv = gelu(uv[:, GMLP_WIDTH:])
    row = lax.broadcasted_iota(jnp.int32, (GMLP_CHUNK, GMLP_CHUNK), 0)
    col = lax.broadcasted_iota(jnp.int32, (GMLP_CHUNK, GMLP_CHUNK), 1)
    tril = col <= row
    for g in range(GMLP_GROUPS):
        sl = slice(g * GMLP_GROUP_DIM, (g + 1) * GMLP_GROUP_DIM)
        vg = gv[:, sl]
        mu = jnp.mean(vg, axis=-1, keepdims=True)
        dv = vg - mu
        var = jnp.mean(dv * dv, axis=-1, keepdims=True)
        vn = (dv * lax.rsqrt(var + NORM_EPS) * lng_ref[:, sl] + lnb_ref[:, sl]).astype(BF16)
        w = jnp.where(tril, ws_ref[g], 0.0).astype(BF16)
        bias = bs_ref[g]
        for c in range(TM // GMLP_CHUNK):
            rs = slice(c * GMLP_CHUNK, (c + 1) * GMLP_CHUNK)
            mixed = jnp.dot(w, vn[rs], preferred_element_type=F32) + bias
            gacc_ref[rs, sl] = gu[rs, sl] * mixed
    gm_ref[...] = _rms(gacc_ref[...], gon_ref[...]).astype(BF16)


def _attn_kernel(q_ref, k_ref, vt_ref, kmean_ref, o_ref, qm_ref, bias_ref):
    n_blk = q_ref.shape[0] // MOBA_BLOCK
    seq = q_ref.shape[0]
    lane = lax.broadcasted_iota(jnp.int32, (1, LANES), 1)
    blk_of_key = lax.broadcasted_iota(jnp.int32, (n_blk, seq), 0)
    blk_of_query = lax.broadcasted_iota(jnp.int32, (n_blk, seq), 1) // MOBA_BLOCK
    key_pos = lax.broadcasted_iota(jnp.int32, (MOBA_BLOCK, MOBA_BLOCK), 0)
    query_pos = lax.broadcasted_iota(jnp.int32, (MOBA_BLOCK, MOBA_BLOCK), 1)
    causal = key_pos <= query_pos

    for h in range(LANES // HEAD_DIM):
        hs = slice(h * HEAD_DIM, (h + 1) * HEAD_DIM)
        in_head = (lane >= h * HEAD_DIM) & (lane < (h + 1) * HEAD_DIM)
        qm = jnp.where(in_head, q_ref[...], jnp.zeros((), BF16))
        qm_ref[...] = qm

        km = jnp.where(in_head, kmean_ref[0], 0.0)
        km1 = km.astype(BF16)
        r1 = km - km1.astype(F32)
        km2 = r1.astype(BF16)
        km3 = (r1 - km2.astype(F32)).astype(BF16)
        gate = _nt(km1, qm) + _nt(km2, qm) + _nt(km3, qm)
        rank = jnp.zeros((n_blk, seq), F32)
        for m in range(n_blk):
            gm_ = gate[m:m + 1, :]
            beats = (m < blk_of_query) & ((gm_ > gate) | ((gm_ == gate) & (m < blk_of_key)))
            rank = rank + jnp.where(beats, 1.0, 0.0)
        chosen = (blk_of_key < blk_of_query) & (rank < MOBA_TOPK)
        bias = jnp.where(chosen, 0.0, NEG_INF)
        for i in range(n_blk):
            bias_ref[i] = bias[:, i * MOBA_BLOCK:(i + 1) * MOBA_BLOCK]

        def query_block(i, _):
            r0 = pl.multiple_of(i * MOBA_BLOCK, MOBA_BLOCK)
            qi = qm_ref[pl.ds(r0, MOBA_BLOCK), :]
            s = jnp.where(causal, _nt(k_ref[pl.ds(r0, MOBA_BLOCK), :], qi), NEG_INF)
            m0 = jnp.max(s, axis=0, keepdims=True)
            p = jnp.exp(s - m0)
            l0 = jnp.sum(p, axis=0, keepdims=True)
            acc0 = jnp.dot(vt_ref[i, hs, :], p.astype(BF16), preferred_element_type=F32)

            def key_block(j, carry):
                m_run, l_run, acc = carry
                c0 = pl.multiple_of(j * MOBA_BLOCK, MOBA_BLOCK)
                sj = _nt(k_ref[pl.ds(c0, MOBA_BLOCK), :], qi) + bias_ref[i, pl.ds(j, 1), :]
                m_new = jnp.maximum(m_run, jnp.max(sj, axis=0, keepdims=True))
                alpha = jnp.exp(m_run - m_new)
                pj = jnp.exp(sj - m_new)
                l_new = alpha * l_run + jnp.sum(pj, axis=0, keepdims=True)
                acc = alpha * acc + jnp.dot(vt_ref[j, hs, :], pj.astype(BF16), preferred_element_type=F32)
                return m_new, l_new, acc

            _, l_fin, acc = lax.fori_loop(0, i, key_block, (m0, l0, acc0))
            o_ref[i, hs, :] = acc / l_fin
            return 0

        lax.fori_loop(0, n_blk, query_block, 0)


def _mix_kernel(x_ref, at_ref, gm_ref, aon_ref, woa_ref, wog_ref, ffn_ref, rwt_ref, rb_ref,
                x1_ref, hn8_ref, topi_ref, gate_ref, rank_ref, cnt_ref, carry_ref):
    @pl.when(pl.program_id(0) == 0)
    def _():
        carry_ref[...] = jnp.zeros_like(carry_ref)

    parts = []
    for c in range(TM // MOBA_BLOCK):
        at = at_ref[c]
        ss = jnp.sum(at * at, axis=0, keepdims=True)
        atn = at * lax.rsqrt(ss * (1.0 / ATTN_WIDTH) + NORM_EPS) * aon_ref[...]
        parts.append(atn.T)
    attn = jnp.concatenate(parts, axis=0).astype(BF16)
    x1 = (x_ref[...] + jnp.dot(attn, woa_ref[...], preferred_element_type=F32)
          + jnp.dot(gm_ref[...], wog_ref[...], preferred_element_type=F32))
    x1_ref[...] = x1
    hn = _rms(x1, ffn_ref[...])
    for j in range(ROW_SLABS):
        hn8_ref[pl.ds(j, TM, stride=SUBLANES), :] = hn[:, j * LANES:(j + 1) * LANES]

    logits = lax.dot_general(rwt_ref[...], hn, (((1,), (1,)), ((), ())),
                             precision=lax.Precision.HIGHEST, preferred_element_type=F32) + rb_ref[...]
    eidx = lax.broadcasted_iota(jnp.int32, (N_EXPERTS, TM), 0)
    vals, idxs = [], []
    rest = logits
    for _ in range(TOP_K):
        m = jnp.max(rest, axis=0, keepdims=True)
        idx = jnp.min(jnp.where(rest == m, eidx, N_EXPERTS), axis=0, keepdims=True)
        vals.append(m)
        idxs.append(idx)
        rest = jnp.where(eidx == idx, -jnp.inf, rest)
    exps = [jnp.exp(v - vals[0]) for v in vals]
    denom = exps[0] + exps[1] + exps[2] + exps[3]
    hot = [eidx == idx for idx in idxs]
    multi = jnp.where(hot[0] | hot[1] | hot[2] | hot[3], 1.0, 0.0)
    tp = lax.broadcasted_iota(jnp.int32, (TM, TM), 0)
    tq = lax.broadcasted_iota(jnp.int32, (TM, TM), 1)
    earlier = jnp.where(tp < tq, 1.0, 0.0).astype(BF16)
    before = jnp.dot(multi.astype(BF16), earlier, preferred_element_type=F32) + carry_ref[:, 0:1]
    for kk in range(TOP_K):
        topi_ref[kk:kk + 1, :] = idxs[kk]
        gate_ref[kk:kk + 1, :] = exps[kk] / denom
        rank_ref[kk:kk + 1, :] = jnp.sum(jnp.where(hot[kk], before, 0.0), axis=0, keepdims=True).astype(jnp.int32)
    carry_ref[...] = carry_ref[...] + jnp.sum(multi, axis=1, keepdims=True)
    cnt_ref[...] = carry_ref[...]


def _row_copy(src_ref, src_row, dst_ref, dst_row, sem):
    return pltpu.make_async_copy(src_ref.at[pl.ds(src_row * SUBLANES, SUBLANES), :],
                                 dst_ref.at[pl.ds(dst_row * SUBLANES, SUBLANES), :], sem)


def _gather_kernel(nused_ref, tok_ref, hn8_ref, xs_ref, buf_ref, sem):
    i = pl.program_id(0)

    @pl.when(i < nused_ref[0])
    def _():
        def start(r, _):
            _row_copy(hn8_ref, tok_ref[0, 0, r], buf_ref, r, sem).start()
            return 0

        lax.fori_loop(0, RB, start, 0)

        def wait(r, _):
            _row_copy(hn8_ref, 0, buf_ref, r, sem).wait()
            return 0

        lax.fori_loop(0, RB, wait, 0)
        for j in range(ROW_SLABS):
            xs_ref[:, j * LANES:(j + 1) * LANES] = buf_ref[pl.ds(j, RB, stride=SUBLANES), :].astype(BF16)

    @pl.when(i >= nused_ref[0])
    def _():
        xs_ref[...] = jnp.zeros_like(xs_ref)


def _expert_kernel(blke_ref, nused_ref, xs_ref, wgu_ref, bgu_ref, wdn_ref, bdn_ref, y8_ref):
    i = pl.program_id(0)

    @pl.when(i < nused_ref[0])
    def _():
        x = xs_ref[...]
        y = jnp.zeros((RB, D_MODEL), F32) + bdn_ref[0]
        for c in range(D_EXPERT // FC):
            g = jnp.dot(x, wgu_ref[0, :, c * FC:(c + 1) * FC], preferred_element_type=F32) \
                + bgu_ref[0, :, c * FC:(c + 1) * FC]
            lin = jnp.dot(x, wgu_ref[0, :, D_EXPERT + c * FC:D_EXPERT + (c + 1) * FC],
                          preferred_element_type=F32) + bgu_ref[0, :, D_EXPERT + c * FC:D_EXPERT + (c + 1) * FC]
            g = jnp.minimum(g, SWIGLU_LIMIT)
            lin = jnp.clip(lin, -SWIGLU_LIMIT, SWIGLU_LIMIT)
            a = g * (1.0 / (1.0 + jnp.exp(-SWIGLU_ALPHA * g))) * (lin + 1.0)
            y = y + jnp.dot(a.astype(BF16), wdn_ref[0, c * FC:(c + 1) * FC, :], preferred_element_type=F32)
        for j in range(ROW_SLABS):
            y8_ref[pl.ds(j, RB, stride=SUBLANES), :] = y[:, j * LANES:(j + 1) * LANES]

    @pl.when(i >= nused_ref[0])
    def _():
        y8_ref[...] = jnp.zeros_like(y8_ref)


def _final_kernel(dest_ref, y8_ref, x1_ref, gate_ref, p_ref, plen_ref, wg_ref, wp_ref, post_ref, fin_ref,
                  o_ref, buf_ref, x2_ref, sem):
    n = TOP_K * TM

    def start(r, _):
        _row_copy(y8_ref, dest_ref[0, 0, r], buf_ref, r, sem).start()
        return 0

    lax.fori_loop(0, n, start, 0)

    def wait(r, _):
        _row_copy(y8_ref, 0, buf_ref, r, sem).wait()
        return 0

    lax.fori_loop(0, n, wait, 0)

    gates = gate_ref[...]
    for j in range(ROW_SLABS):
        ls = slice(j * LANES, (j + 1) * LANES)
        acc = x1_ref[:, ls]
        for kk in range(TOP_K):
            acc = acc + gates[:, kk:kk + 1] * buf_ref[pl.ds(kk * TM * SUBLANES + j, TM, stride=SUBLANES), :]
        x2_ref[:, ls] = acc
    x2 = x2_ref[...]
    z = jnp.dot(_rms(x2, plen_ref[...]).astype(BF16), wg_ref[...], preferred_element_type=F32)
    ple_gate = 1.0 / (1.0 + jnp.exp(-z))
    emb = jnp.dot(p_ref[...].astype(BF16), wp_ref[...], preferred_element_type=F32)
    x3 = x2 + ple_gate * _rms(emb, post_ref[...])
    o_ref[...] = _rms(x3, fin_ref[...])


def _params(sem):
    return pltpu.CompilerParams(dimension_semantics=sem, vmem_limit_bytes=VMEM_LIMIT)


def _rope_tables(seq):
    half = HEAD_DIM // 2
    inv = ROPE_THETA ** (-jnp.arange(half, dtype=F32) / half)
    ang = jnp.arange(seq, dtype=F32)[:, None] * inv[None, :]
    cos = jnp.cos(ang)
    sin = jnp.sin(ang)
    n_heads = ATTN_WIDTH // HEAD_DIM
    cos_t = jnp.tile(jnp.concatenate([cos, cos], axis=1), (1, n_heads))
    sin_t = jnp.tile(jnp.concatenate([-sin, sin], axis=1), (1, n_heads))
    return cos_t, sin_t


def _layer(x, p, mix_norm, w_in, gmlp_ln_g, gmlp_ln_b, gmlp_ws, gmlp_bs, attn_out_norm, gmlp_out_norm,
           w_out, ffn_norm, router_w, router_b, w_gate_up, b_gate_up, w_down, b_down, ple_norm,
           w_ple_gate, w_ple_proj, ple_post_norm, out_norm):
    B, S, D = x.shape
    T = B * S
    A = ATTN_WIDTH
    n_tiles = T // TM
    n_blk = S // MOBA_BLOCK
    tiles_per_seq = S // TM
    xf = x.reshape(T, D)
    row2 = lambda v: v.reshape(1, -1)
    full = lambda shape: pl.BlockSpec(shape, lambda *_: (0,) * len(shape))

    cos_t, sin_t = _rope_tables(S)
    wqk = w_in[:, :2 * A].astype(BF16)
    wvt = w_in[:, 2 * A:3 * A].T.astype(BF16)
    wuv = w_in[:, 3 * A:].astype(BF16)

    q, k, vt, kmean, gm = pl.pallas_call(
        _proj_kernel,
        grid=(n_tiles,),
        in_specs=[pl.BlockSpec((TM, D), lambda i: (i, 0)), full((1, D)), full((D, 2 * A)), full((A, D)),
                  full((D, 2 * GMLP_WIDTH)),
                  pl.BlockSpec((TM, A), lambda i: (i % tiles_per_seq, 0)),
                  pl.BlockSpec((TM, A), lambda i: (i % tiles_per_seq, 0)),
                  full((1, GMLP_WIDTH)), full((1, GMLP_WIDTH)),
                  full((GMLP_GROUPS, GMLP_CHUNK, GMLP_CHUNK)), full((GMLP_GROUPS, GMLP_CHUNK, 1)),
                  full((1, GMLP_WIDTH))],
        out_specs=[pl.BlockSpec((TM, A), lambda i: (i, 0)), pl.BlockSpec((TM, A), lambda i: (i, 0)),
                   pl.BlockSpec((TM // MOBA_BLOCK, A, MOBA_BLOCK), lambda i: (i, 0, 0)),
                   pl.BlockSpec((1, TM // MOBA_BLOCK, A), lambda i: (i, 0, 0)),
                   pl.BlockSpec((TM, GMLP_WIDTH), lambda i: (i, 0))],
        out_shape=[jax.ShapeDtypeStruct((T, A), BF16), jax.ShapeDtypeStruct((T, A), BF16),
                   jax.ShapeDtypeStruct((T // MOBA_BLOCK, A, MOBA_BLOCK), BF16),
                   jax.ShapeDtypeStruct((n_tiles, TM // MOBA_BLOCK, A), F32),
                   jax.ShapeDtypeStruct((T, GMLP_WIDTH), BF16)],
        scratch_shapes=[pltpu.VMEM((TM, GMLP_WIDTH), F32)],
        compiler_params=_params(("parallel",)),
        name="proj",
    )(xf, row2(mix_norm), wqk, wvt, wuv, cos_t, sin_t, row2(gmlp_ln_g), row2(gmlp_ln_b), gmlp_ws,
      gmlp_bs.reshape(GMLP_GROUPS, GMLP_CHUNK, 1), row2(gmlp_out_norm))

    kmean = kmean.reshape(B, n_blk, A)
    n_pairs = A // LANES
    attn_t = pl.pallas_call(
        _attn_kernel,
        grid=(B, n_pairs),
        in_specs=[pl.BlockSpec((S, LANES), lambda b, h: (b, h)), pl.BlockSpec((S, LANES), lambda b, h: (b, h)),
                  pl.BlockSpec((n_blk, LANES, MOBA_BLOCK), lambda b, h: (b, h, 0)),
                  pl.BlockSpec((1, n_blk, LANES), lambda b, h: (b, 0, h))],
        out_specs=pl.BlockSpec((n_blk, LANES, MOBA_BLOCK), lambda b, h: (b, h, 0)),
        out_shape=jax.ShapeDtypeStruct((T // MOBA_BLOCK, A, MOBA_BLOCK), F32),
        scratch_shapes=[pltpu.VMEM((S, LANES), BF16), pltpu.VMEM((n_blk, n_blk, MOBA_BLOCK), F32)],
        compiler_params=_params(("parallel", "parallel")),
        name="attn",
    )(q, k, vt, kmean)

    x1, hn8, topi, gates, rank, cnt = pl.pallas_call(
        _mix_kernel,
        grid=(n_tiles,),
        in_specs=[pl.BlockSpec((TM, D), lambda i: (i, 0)),
                  pl.BlockSpec((TM // MOBA_BLOCK, A, MOBA_BLOCK), lambda i: (i, 0, 0)),
                  pl.BlockSpec((TM, GMLP_WIDTH), lambda i: (i, 0)),
                  full((A, 1)), full((A, D)), full((GMLP_WIDTH, D)), full((1, D)),
                  full((N_EXPERTS, D)), full((N_EXPERTS, 1))],
        out_specs=[pl.BlockSpec((TM, D), lambda i: (i, 0)),
                   pl.BlockSpec((TM * SUBLANES, LANES), lambda i: (i, 0)),
                   pl.BlockSpec((TOP_K, TM), lambda i: (0, i)), pl.BlockSpec((TOP_K, TM), lambda i: (0, i)),
                   pl.BlockSpec((TOP_K, TM), lambda i: (0, i)), full((N_EXPERTS, LANES))],
        out_shape=[jax.ShapeDtypeStruct((T, D), F32), jax.ShapeDtypeStruct((T * SUBLANES, LANES), F32),
                   jax.ShapeDtypeStruct((TOP_K, T), jnp.int32), jax.ShapeDtypeStruct((TOP_K, T), F32),
                   jax.ShapeDtypeStruct((TOP_K, T), jnp.int32), jax.ShapeDtypeStruct((N_EXPERTS, LANES), F32)],
        scratch_shapes=[pltpu.VMEM((N_EXPERTS, LANES), F32)],
        compiler_params=_params(("arbitrary",)),
        name="mix",
    )(xf, attn_t, gm, attn_out_norm.reshape(A, 1), w_out[:A].astype(BF16), w_out[A:].astype(BF16),
      row2(ffn_norm), router_w.T, router_b.reshape(N_EXPERTS, 1))

    counts = cnt[:, 0].astype(jnp.int32)
    padded = (counts + RB - 1) // RB * RB
    pend = jnp.cumsum(padded)
    pstart = pend - padded
    dest = pstart[topi] + rank
    n_rows = T * TOP_K + N_EXPERTS * RB
    n_blocks = n_rows // RB
    tok_ids = jnp.broadcast_to(jnp.arange(T, dtype=jnp.int32)[None, :], (TOP_K, T))
    row_tok = jnp.zeros((n_rows,), jnp.int32).at[dest.reshape(-1)].set(tok_ids.reshape(-1))
    blk_e = jnp.minimum(jnp.searchsorted(pend, jnp.arange(n_blocks, dtype=jnp.int32) * RB, side='right'),
                        N_EXPERTS - 1).astype(jnp.int32)
    nused = (pend[-1:] // RB).astype(jnp.int32)

    xs = pl.pallas_call(
        _gather_kernel,
        grid_spec=pltpu.PrefetchScalarGridSpec(
            num_scalar_prefetch=1, grid=(n_blocks,),
            in_specs=[pl.BlockSpec((1, 1, RB), lambda i, nu: (i, 0, 0), memory_space=pltpu.SMEM),
                      pl.BlockSpec(memory_space=pl.ANY)],
            out_specs=pl.BlockSpec((RB, D), lambda i, nu: (i, 0)),
            scratch_shapes=[pltpu.VMEM((RB * SUBLANES, LANES), F32), pltpu.SemaphoreType.DMA(())]),
        out_shape=jax.ShapeDtypeStruct((n_rows, D), BF16),
        compiler_params=_params(("arbitrary",)),
        name="gather",
    )(nused, row_tok.reshape(n_blocks, 1, RB), hn8)

    last = lambda i, nu: jnp.minimum(i, nu[0] - 1)
    y8 = pl.pallas_call(
        _expert_kernel,
        grid_spec=pltpu.PrefetchScalarGridSpec(
            num_scalar_prefetch=2, grid=(n_blocks,),
            in_specs=[pl.BlockSpec((RB, D), lambda i, be, nu: (last(i, nu), 0)),
                      pl.BlockSpec((1, D, 2 * D_EXPERT), lambda i, be, nu: (be[last(i, nu)], 0, 0)),
                      pl.BlockSpec((1, 1, 2 * D_EXPERT), lambda i, be, nu: (be[last(i, nu)], 0, 0)),
                      pl.BlockSpec((1, D_EXPERT, D), lambda i, be, nu: (be[last(i, nu)], 0, 0)),
                      pl.BlockSpec((1, 1, D), lambda i, be, nu: (be[last(i, nu)], 0, 0))],
            out_specs=pl.BlockSpec((RB * SUBLANES, LANES), lambda i, be, nu: (i, 0))),
        out_shape=jax.ShapeDtypeStruct((n_rows * SUBLANES, LANES), F32),
        compiler_params=_params(("arbitrary",)),
        name="experts",
    )(blk_e, nused, xs, w_gate_up.astype(BF16), b_gate_up.reshape(N_EXPERTS, 1, -1),
      w_down.astype(BF16), b_down.reshape(N_EXPERTS, 1, -1))

    dest_tiles = dest.reshape(TOP_K, n_tiles, TM).transpose(1, 0, 2).reshape(n_tiles, 1, TOP_K * TM)
    out = pl.pallas_call(
        _final_kernel,
        grid=(n_tiles,),
        in_specs=[pl.BlockSpec((1, 1, TOP_K * TM), lambda i: (i, 0, 0), memory_space=pltpu.SMEM),
                  pl.BlockSpec(memory_space=pl.ANY),
                  pl.BlockSpec((TM, D), lambda i: (i, 0)), pl.BlockSpec((TM, TOP_K), lambda i: (i, 0)),
                  pl.BlockSpec((TM, PLE_DIM), lambda i: (i, 0)), full((1, D)), full((D, D)), full((PLE_DIM, D)),
                  full((1, D)), full((1, D))],
        out_specs=pl.BlockSpec((TM, D), lambda i: (i, 0)),
        out_shape=jax.ShapeDtypeStruct((T, D), F32),
        scratch_shapes=[pltpu.VMEM((TOP_K * TM * SUBLANES, LANES), F32), pltpu.VMEM((TM, D), F32),
                        pltpu.SemaphoreType.DMA(())],
        compiler_params=_params(("arbitrary",)),
        name="final",
    )(dest_tiles, y8, x1, gates.T, p.reshape(T, PLE_DIM), row2(ple_norm), w_ple_gate.astype(BF16),
      w_ple_proj.astype(BF16), row2(ple_post_norm), row2(out_norm))
    return out.reshape(B, S, D)


def kernel(x, p, mix_norm, w_in, gmlp_ln_g, gmlp_ln_b, gmlp_ws, gmlp_bs, attn_out_norm, gmlp_out_norm, w_out, ffn_norm, router_w, router_b, w_gate_up, b_gate_up, w_down, b_down, ple_norm, w_ple_gate, w_ple_proj, ple_post_norm, final_norm):
    depth = p.shape[0]
    assert depth == 1, "the final rmsnorm is fused into the (single) layer"
    return _layer(x, p[0], mix_norm[0], w_in[0], gmlp_ln_g[0], gmlp_ln_b[0], gmlp_ws[0], gmlp_bs[0],
                  attn_out_norm[0], gmlp_out_norm[0], w_out[0], ffn_norm[0], router_w[0], router_b[0],
                  w_gate_up[0], b_gate_up[0], w_down[0], b_down[0], ple_norm[0], w_ple_gate[0],
                  w_ple_proj[0], ple_post_norm[0], final_norm)
```

```python
import jax
import jax.numpy as jnp
import numpy as np
from jax import lax
from jax.experimental import pallas as pl
from jax.experimental.pallas import tpu as pltpu

F32 = jnp.float32
BF16 = jnp.bfloat16
I32 = jnp.int32

D_MODEL = 1024
PLE_DIM = 256
ATTN_WIDTH = 512
HEAD_DIM = 64
ROPE_THETA = 10000.0
MOBA_BLOCK = 256
MOBA_TOPK = 3
GMLP_WIDTH = 512
GMLP_GROUPS = 4
GMLP_GROUP_DIM = 128
GMLP_CHUNK = 128
N_EXPERTS = 32
TOP_K = 4
D_EXPERT = 1024
SWIGLU_LIMIT = 7.0
SWIGLU_ALPHA = 1.702
NORM_EPS = 1e-6
NEG_INF = -1e30

LANES = 128
CHUNK = 16

TM = 512
RB = 512
FC = 512
PAIRS = TM * TOP_K
SORT_ROWS = 2560
TAB_SEG, TAB_OFF, TAB_DST = 0, N_EXPERTS, 2 * N_EXPERTS
VMEM_LIMIT = 48 * 1024 * 1024

assert SORT_ROWS % TM == 0 and SORT_ROWS >= PAIRS + N_EXPERTS * (CHUNK - 1)


def _rms(x, g):
    return x * lax.rsqrt(jnp.mean(x * x, axis=-1, keepdims=True) + NORM_EPS) * g


def _nt(a, b):
    return lax.dot_general(a, b, (((1,), (1,)), ((), ())), preferred_element_type=F32)


def _proj_kernel(x_ref, mixn_ref, wk_ref, wqvt_ref, wuv_ref, cos_ref, sin_ref, cost_ref, sint_ref, lng_ref,
                 lnb_ref, ws_ref, bs_ref, gon_ref, qt_ref, k_ref, vt_ref, kmean_ref, gm_ref, gacc_ref):
    h = _rms(x_ref[...], mixn_ref[...]).astype(BF16)
    half = HEAD_DIM // 2

    kx = jnp.dot(h, wk_ref[...], preferred_element_type=F32)
    lane = lax.broadcasted_iota(I32, (TM, LANES), 1)
    first_half = (lane % HEAD_DIM) < half
    parts = []
    for s in range(ATTN_WIDTH // LANES):
        slab = kx[:, s * LANES:(s + 1) * LANES]
        ahead = pltpu.roll(slab, LANES - half, 1)
        behind = pltpu.roll(slab, half, 1)
        parts.append(jnp.where(first_half, ahead, behind))
    k = kx * cos_ref[...] + jnp.concatenate(parts, axis=1) * sin_ref[...]
    k_ref[...] = k.astype(BF16)
    for c in range(TM // MOBA_BLOCK):
        kmean_ref[0, c:c + 1, :] = jnp.mean(k[c * MOBA_BLOCK:(c + 1) * MOBA_BLOCK], axis=0, keepdims=True)

    qvt = _nt(wqvt_ref[...], h)
    qx = qvt[:ATTN_WIDTH]
    swapped = []
    for hd in range(ATTN_WIDTH // HEAD_DIM):
        r0 = hd * HEAD_DIM
        swapped += [qx[r0 + half:r0 + HEAD_DIM], qx[r0:r0 + half]]
    qt = (qx * cost_ref[...] + jnp.concatenate(swapped, axis=0) * sint_ref[...]) * (1.0 / np.sqrt(HEAD_DIM))
    qt_ref[...] = qt.astype(BF16)
    vt_ref[...] = qvt[ATTN_WIDTH:].astype(BF16)

    uv = jnp.dot(h, wuv_ref[...], preferred_element_type=F32)
    inv_sqrt2 = np.float32(1.0 / np.sqrt(2.0))

    def gelu(t):
        return 0.5 * t * (1.0 + lax.erf(t * inv_sqrt2))

    gu = gelu(uv[:, :GMLP_WIDTH])
    gv = gelu(uv[:, GMLP_WIDTH:])
    row = lax.broadcasted_iota(I32, (GMLP_CHUNK, GMLP_CHUNK), 0)
    col = lax.broadcasted_iota(I32, (GMLP_CHUNK, GMLP_CHUNK), 1)
    tril = col <= row
    for g in range(GMLP_GROUPS):
        sl = slice(g * GMLP_GROUP_DIM, (g + 1) * GMLP_GROUP_DIM)
        vg = gv[:, sl]
        mu = jnp.mean(vg, axis=-1, keepdims=True)
        dv = vg - mu
        var = jnp.mean(dv * dv, axis=-1, keepdims=True)
        vn = (dv * lax.rsqrt(var + NORM_EPS) * lng_ref[:, sl] + lnb_ref[:, sl]).astype(BF16)
        w = jnp.where(tril, ws_ref[g], 0.0).astype(BF16)
        bias = bs_ref[g]
        for c in range(TM // GMLP_CHUNK):
            rs = slice(c * GMLP_CHUNK, (c + 1) * GMLP_CHUNK)
            mixed = jnp.dot(w, vn[rs], preferred_element_type=F32) + bias
            gacc_ref[rs, sl] = gu[rs, sl] * mixed
    gm_ref[...] = _rms(gacc_ref[...], gon_ref[...]).astype(BF16)


def _attn_kernel(qt_ref, k_ref, vt_ref, kmean_ref, o_ref):
    seq = k_ref.shape[0]
    n_blk = seq // MOBA_BLOCK
    feat = lax.broadcasted_iota(I32, (LANES, 1), 0)
    lane = lax.broadcasted_iota(I32, (1, LANES), 1)
    blk_of_key = lax.broadcasted_iota(I32, (n_blk, seq), 0)
    blk_of_query = lax.broadcasted_iota(I32, (n_blk, seq), 1) // MOBA_BLOCK
    key_pos = lax.broadcasted_iota(I32, (MOBA_BLOCK, MOBA_BLOCK), 0)
    query_pos = lax.broadcasted_iota(I32, (MOBA_BLOCK, MOBA_BLOCK), 1)
    causal = key_pos <= query_pos
    blk = lambda i: slice(i * MOBA_BLOCK, (i + 1) * MOBA_BLOCK)

    for h in range(LANES // HEAD_DIM):
        hs = slice(h * HEAD_DIM, (h + 1) * HEAD_DIM)
        rows_in_head = (feat >= h * HEAD_DIM) & (feat < (h + 1) * HEAD_DIM)
        qt = jnp.where(rows_in_head, qt_ref[...], jnp.zeros((), BF16))

        km = jnp.where((lane >= h * HEAD_DIM) & (lane < (h + 1) * HEAD_DIM), kmean_ref[0], 0.0)
        km1 = km.astype(BF16)
        r1 = km - km1.astype(F32)
        km2 = r1.astype(BF16)
        km3 = (r1 - km2.astype(F32)).astype(BF16)
        dotf = lambda a, b: jnp.dot(a, b, preferred_element_type=F32)
        gate = dotf(km1, qt) + dotf(km2, qt) + dotf(km3, qt)
        rank = jnp.zeros((n_blk, seq), F32)
        for m in range(n_blk):
            gm_ = gate[m:m + 1, :]
            beats = (m < blk_of_query) & ((gm_ > gate) | ((gm_ == gate) & (m < blk_of_key)))
            rank = rank + jnp.where(beats, 1.0, 0.0)
        chosen = (blk_of_key < blk_of_query) & (rank < MOBA_TOPK)
        bias = jnp.where(chosen, 0.0, NEG_INF)

        for i in range(n_blk):
            n_keys = (i + 1) * MOBA_BLOCK
            s = dotf(k_ref[0:n_keys, :], qt[:, blk(i)])
            tiles = [s[blk(j)] + bias[j:j + 1, blk(i)] for j in range(i)]
            tiles.append(jnp.where(causal, s[blk(i)], NEG_INF))
            top = tiles[0]
            for t in tiles[1:]:
                top = jnp.maximum(top, t)
            m_q = jnp.max(top, axis=0, keepdims=True)
            probs = [jnp.exp(t - m_q) for t in tiles]
            total = probs[0]
            for pj in probs[1:]:
                total = total + pj
            inv = 1.0 / jnp.sum(total, axis=0, keepdims=True)
            pn = jnp.concatenate([(pj * inv).astype(BF16) for pj in probs], axis=0)
            o_ref[i, hs, :] = dotf(vt_ref[hs, 0:n_keys], pn)


def _mix_kernel(x_ref, at_ref, gm_ref, aon_ref, woa_ref, wog_ref, ffn_ref, rwt_ref, rb_ref,
                x1_ref, hn_ref, lpos_ref, gate_ref, tab_ref, tot_ref, carry_ref):
    @pl.when(pl.program_id(0) == 0)
    def _():
        carry_ref[...] = jnp.zeros_like(carry_ref)

    parts = []
    for c in range(TM // MOBA_BLOCK):
        at = at_ref[c]
        ss = jnp.sum(at * at, axis=0, keepdims=True)
        atn = at * lax.rsqrt(ss * (1.0 / ATTN_WIDTH) + NORM_EPS) * aon_ref[...]
        parts.append(atn.T)
    attn = jnp.concatenate(parts, axis=0).astype(BF16)
    x1 = (x_ref[...] + jnp.dot(attn, woa_ref[...], preferred_element_type=F32)
          + jnp.dot(gm_ref[...], wog_ref[...], preferred_element_type=F32))
    x1_ref[...] = x1
    hn = _rms(x1, ffn_ref[...])
    hn_ref[...] = hn.astype(BF16)

    logits = lax.dot_general(rwt_ref[...], hn, (((1,), (1,)), ((), ())),
                             precision=lax.Precision.HIGHEST, preferred_element_type=F32) + rb_ref[...]
    eidx = lax.broadcasted_iota(I32, (N_EXPERTS, TM), 0)
    vals, idxs = [], []
    rest = logits
    for _ in range(TOP_K):
        m = jnp.max(rest, axis=0, keepdims=True)
        idx = jnp.min(jnp.where(rest == m, eidx, N_EXPERTS), axis=0, keepdims=True)
        vals.append(m)
        idxs.append(idx)
        rest = jnp.where(eidx == idx, -jnp.inf, rest)
    exps = [jnp.exp(v - vals[0]) for v in vals]
    denom = exps[0] + exps[1] + exps[2] + exps[3]
    hot = [eidx == idx for idx in idxs]
    multi = jnp.where(hot[0] | hot[1] | hot[2] | hot[3], 1.0, 0.0)

    tp = lax.broadcasted_iota(I32, (TM, TM), 0)
    tq = lax.broadcasted_iota(I32, (TM, TM), 1)
    earlier = jnp.where(tp < tq, 1.0, 0.0).astype(BF16)
    before = jnp.dot(multi.astype(BF16), earlier, preferred_element_type=F32)
    count = jnp.sum(multi, axis=1, keepdims=True)
    seg = jnp.floor((count + (CHUNK - 1)) * (1.0 / CHUNK))
    ep = lax.broadcasted_iota(I32, (N_EXPERTS, N_EXPERTS), 0)
    eq = lax.broadcasted_iota(I32, (N_EXPERTS, N_EXPERTS), 1)
    lower = jnp.where(eq < ep, 1.0, 0.0).astype(BF16)
    seg_wide = jnp.broadcast_to(seg, (N_EXPERTS, LANES))
    off = jnp.dot(lower, seg_wide.astype(BF16), preferred_element_type=F32)
    slot = off[:, 0:1] * CHUNK + before
    for kk in range(TOP_K):
        lpos_ref[kk:kk + 1, :] = jnp.sum(jnp.where(hot[kk], slot, 0.0), axis=0, keepdims=True).astype(I32)
        gate_ref[kk:kk + 1, :] = exps[kk] / denom
    lane = lax.broadcasted_iota(I32, (N_EXPERTS, LANES), 1)
    tab = jnp.where(lane == 0, seg_wide, jnp.where(lane == 1, off, carry_ref[...]))
    tab_ref[0] = tab.astype(I32)
    carry_ref[...] = carry_ref[...] + seg
    tot_ref[...] = carry_ref[...].astype(I32)


def _chunk_rows(ref, chunk_index):
    return ref.at[pl.ds(pl.multiple_of(chunk_index * CHUNK, CHUNK), CHUNK), :]


def _copy_segments(tab_ref, sorted_ref, rows_ref, sem, to_rows):
    def per_expert(e, started):
        n = tab_ref[0, 0, TAB_SEG + e]
        off = tab_ref[0, 0, TAB_OFF + e]
        dst = tab_ref[0, 0, TAB_DST + e]

        def per_chunk(c, _):
            a, b = _chunk_rows(sorted_ref, off + c), _chunk_rows(rows_ref, dst + c)
            (pltpu.make_async_copy(a, b, sem) if to_rows else pltpu.make_async_copy(b, a, sem)).start()
            return 0

        lax.fori_loop(0, n, per_chunk, 0)
        return started + n

    return lax.fori_loop(0, N_EXPERTS, per_expert, 0)


def _wait_chunks(n, a_ref, b_ref, sem):
    def wait(_, carry):
        pltpu.make_async_copy(_chunk_rows(a_ref, 0), _chunk_rows(b_ref, 0), sem).wait()
        return carry

    lax.fori_loop(0, n, wait, 0)


def _dispatch_kernel(fill_start_ref, fill_n_ref, tab_ref, hn_ref, lpos_ref, xs_ref, sorted_ref, zero_ref, sem):
    hn = hn_ref[...]
    lpos = lpos_ref[...]
    for c in range(SORT_ROWS // TM):
        r = lax.broadcasted_iota(I32, (TM, TM), 0) + c * TM
        hit = (r == lpos[0:1]) | (r == lpos[1:2]) | (r == lpos[2:3]) | (r == lpos[3:4])
        onehot = jnp.where(hit, 1.0, 0.0).astype(BF16)
        sorted_ref[c * TM:(c + 1) * TM, :] = jnp.dot(onehot, hn, preferred_element_type=F32).astype(BF16)
    started = _copy_segments(tab_ref, sorted_ref, xs_ref, sem, to_rows=True)
    _wait_chunks(started, sorted_ref, xs_ref, sem)

    @pl.when(pl.program_id(0) == pl.num_programs(0) - 1)
    def _():
        zero_ref[...] = jnp.zeros_like(zero_ref)

        def per_range(e, started):
            n = fill_n_ref[e]
            base = fill_start_ref[e]

            def per_chunk(c, _):
                pltpu.make_async_copy(zero_ref, _chunk_rows(xs_ref, base + c), sem).start()
                return 0

            lax.fori_loop(0, n, per_chunk, 0)
            return started + n

        n_fill = lax.fori_loop(0, N_EXPERTS + 1, per_range, 0)
        _wait_chunks(n_fill, zero_ref, xs_ref, sem)


def _expert_kernel(blke_ref, nused_ref, xs_ref, wgu_ref, bgu_ref, wdn_ref, bdn_ref, y_ref):
    i = pl.program_id(0)

    @pl.when(i < nused_ref[0])
    def _():
        x = xs_ref[...]
        y = jnp.zeros((RB, D_MODEL), F32) + bdn_ref[0]
        for c in range(D_EXPERT // FC):
            g = jnp.dot(x, wgu_ref[0, :, c * FC:(c + 1) * FC], preferred_element_type=F32) \
                + bgu_ref[0, :, c * FC:(c + 1) * FC]
            lin = jnp.dot(x, wgu_ref[0, :, D_EXPERT + c * FC:D_EXPERT + (c + 1) * FC],
                          preferred_element_type=F32) + bgu_ref[0, :, D_EXPERT + c * FC:D_EXPERT + (c + 1) * FC]
            g = jnp.minimum(g, SWIGLU_LIMIT)
            lin = jnp.clip(lin, -SWIGLU_LIMIT, SWIGLU_LIMIT)
            a = g * (1.0 / (1.0 + jnp.exp(-SWIGLU_ALPHA * g))) * (lin + 1.0)
            y = y + jnp.dot(a.astype(BF16), wdn_ref[0, c * FC:(c + 1) * FC, :], preferred_element_type=F32)
        y_ref[...] = y.astype(BF16)

    @pl.when(i >= nused_ref[0])
    def _():
        y_ref[...] = jnp.zeros_like(y_ref)


def _final_kernel(tab_ref, y_ref, x1_ref, lpos_ref, gate_ref, p_ref, plen_ref, wg_ref, wp_ref, post_ref, fin_ref,
                  o_ref, sorted_ref, sem):
    sorted_ref[PAIRS:, :] = jnp.zeros((SORT_ROWS - PAIRS, D_MODEL), BF16)
    started = _copy_segments(tab_ref, sorted_ref, y_ref, sem, to_rows=False)
    _wait_chunks(started, y_ref, sorted_ref, sem)

    lpos = lpos_ref[...]
    gates = gate_ref[...]
    moe = jnp.zeros((TM, D_MODEL), F32)
    for c in range(SORT_ROWS // TM):
        r = lax.broadcasted_iota(I32, (TM, TM), 1) + c * TM
        w = jnp.zeros((TM, TM), F32)
        for kk in range(TOP_K):
            w = w + jnp.where(r == lpos[:, kk:kk + 1], gates[:, kk:kk + 1], 0.0)
        moe = moe + jnp.dot(w.astype(BF16), sorted_ref[c * TM:(c + 1) * TM, :], preferred_element_type=F32)
    x2 = x1_ref[...] + moe
    z = jnp.dot(_rms(x2, plen_ref[...]).astype(BF16), wg_ref[...], preferred_element_type=F32)
    ple_gate = 1.0 / (1.0 + jnp.exp(-z))
    emb = jnp.dot(p_ref[...].astype(BF16), wp_ref[...], preferred_element_type=F32)
    x3 = x2 + ple_gate * _rms(emb, post_ref[...])
    o_ref[...] = _rms(x3, fin_ref[...])


def _params(sem):
    return pltpu.CompilerParams(dimension_semantics=sem, vmem_limit_bytes=VMEM_LIMIT)


def _rope_tables(seq):
    half = HEAD_DIM // 2
    inv = ROPE_THETA ** (-jnp.arange(half, dtype=F32) / half)
    ang = jnp.arange(seq, dtype=F32)[:, None] * inv[None, :]
    cos = jnp.cos(ang)
    sin = jnp.sin(ang)
    n_heads = ATTN_WIDTH // HEAD_DIM
    cos_t = jnp.tile(jnp.concatenate([cos, cos], axis=1), (1, n_heads))
    sin_t = jnp.tile(jnp.concatenate([-sin, sin], axis=1), (1, n_heads))
    return cos_t, sin_t


def _layer(x, p, mix_norm, w_in, gmlp_ln_g, gmlp_ln_b, gmlp_ws, gmlp_bs, attn_out_norm, gmlp_out_norm,
           w_out, ffn_norm, router_w, router_b, w_gate_up, b_gate_up, w_down, b_down, ple_norm,
           w_ple_gate, w_ple_proj, ple_post_norm, out_norm):
    B, S, D = x.shape
    T = B * S
    A = ATTN_WIDTH
    n_tiles = T // TM
    n_blk = S // MOBA_BLOCK
    tiles_per_seq = S // TM
    xf = x.reshape(T, D)
    row2 = lambda v: v.reshape(1, -1)
    full = lambda shape: pl.BlockSpec(shape, lambda *_: (0,) * len(shape))

    cos_t, sin_t = _rope_tables(S)
    wk = w_in[:, A:2 * A].astype(BF16)
    wqvt = jnp.concatenate([w_in[:, :A], w_in[:, 2 * A:3 * A]], axis=1).T.astype(BF16)
    wuv = w_in[:, 3 * A:].astype(BF16)

    qt, k, vt, kmean, gm = pl.pallas_call(
        _proj_kernel,
        grid=(n_tiles,),
        in_specs=[pl.BlockSpec((TM, D), lambda i: (i, 0)), full((1, D)), full((D, A)), full((2 * A, D)),
                  full((D, 2 * GMLP_WIDTH)),
                  pl.BlockSpec((TM, A), lambda i: (i % tiles_per_seq, 0)),
                  pl.BlockSpec((TM, A), lambda i: (i % tiles_per_seq, 0)),
                  pl.BlockSpec((A, TM), lambda i: (0, i % tiles_per_seq)),
                  pl.BlockSpec((A, TM), lambda i: (0, i % tiles_per_seq)),
                  full((1, GMLP_WIDTH)), full((1, GMLP_WIDTH)),
                  full((GMLP_GROUPS, GMLP_CHUNK, GMLP_CHUNK)), full((GMLP_GROUPS, GMLP_CHUNK, 1)),
                  full((1, GMLP_WIDTH))],
        out_specs=[pl.BlockSpec((A, TM), lambda i: (0, i)), pl.BlockSpec((TM, A), lambda i: (i, 0)),
                   pl.BlockSpec((A, TM), lambda i: (0, i)),
                   pl.BlockSpec((1, TM // MOBA_BLOCK, A), lambda i: (i, 0, 0)),
                   pl.BlockSpec((TM, GMLP_WIDTH), lambda i: (i, 0))],
        out_shape=[jax.ShapeDtypeStruct((A, T), BF16), jax.ShapeDtypeStruct((T, A), BF16),
                   jax.ShapeDtypeStruct((A, T), BF16),
                   jax.ShapeDtypeStruct((n_tiles, TM // MOBA_BLOCK, A), F32),
                   jax.ShapeDtypeStruct((T, GMLP_WIDTH), BF16)],
        scratch_shapes=[pltpu.VMEM((TM, GMLP_WIDTH), F32)],
        compiler_params=_params(("parallel",)),
        name="proj",
    )(xf, row2(mix_norm), wk, wqvt, wuv, cos_t, sin_t, cos_t.T, sin_t.T, row2(gmlp_ln_g), row2(gmlp_ln_b), gmlp_ws,
      gmlp_bs.reshape(GMLP_GROUPS, GMLP_CHUNK, 1), row2(gmlp_out_norm))

    kmean = kmean.reshape(B, n_blk, A)
    n_pairs = A // LANES
    attn_t = pl.pallas_call(
        _attn_kernel,
        grid=(B, n_pairs),
        in_specs=[pl.BlockSpec((LANES, S), lambda b, h: (h, b)), pl.BlockSpec((S, LANES), lambda b, h: (b, h)),
                  pl.BlockSpec((LANES, S), lambda b, h: (h, b)),
                  pl.BlockSpec((1, n_blk, LANES), lambda b, h: (b, 0, h))],
        out_specs=pl.BlockSpec((n_blk, LANES, MOBA_BLOCK), lambda b, h: (b, h, 0)),
        out_shape=jax.ShapeDtypeStruct((T // MOBA_BLOCK, A, MOBA_BLOCK), F32),
        compiler_params=_params(("parallel", "parallel")),
        name="attn",
    )(qt, k, vt, kmean)

    x1, hn, lpos, gates, tab, tot = pl.pallas_call(
        _mix_kernel,
        grid=(n_tiles,),
        in_specs=[pl.BlockSpec((TM, D), lambda i: (i, 0)),
                  pl.BlockSpec((TM // MOBA_BLOCK, A, MOBA_BLOCK), lambda i: (i, 0, 0)),
                  pl.BlockSpec((TM, GMLP_WIDTH), lambda i: (i, 0)),
                  full((A, 1)), full((A, D)), full((GMLP_WIDTH, D)), full((1, D)),
                  full((N_EXPERTS, D)), full((N_EXPERTS, 1))],
        out_specs=[pl.BlockSpec((TM, D), lambda i: (i, 0)), pl.BlockSpec((TM, D), lambda i: (i, 0)),
                   pl.BlockSpec((TOP_K, TM), lambda i: (0, i)), pl.BlockSpec((TOP_K, TM), lambda i: (0, i)),
                   pl.BlockSpec((1, N_EXPERTS, LANES), lambda i: (i, 0, 0)), full((N_EXPERTS, LANES))],
        out_shape=[jax.ShapeDtypeStruct((T, D), F32), jax.ShapeDtypeStruct((T, D), BF16),
                   jax.ShapeDtypeStruct((TOP_K, T), I32), jax.ShapeDtypeStruct((TOP_K, T), F32),
                   jax.ShapeDtypeStruct((n_tiles, N_EXPERTS, LANES), I32),
                   jax.ShapeDtypeStruct((N_EXPERTS, LANES), I32)],
        scratch_shapes=[pltpu.VMEM((N_EXPERTS, LANES), F32)],
        compiler_params=_params(("arbitrary",)),
        name="mix",
    )(xf, attn_t, gm, attn_out_norm.reshape(A, 1), w_out[:A].astype(BF16), w_out[A:].astype(BF16),
      row2(ffn_norm), router_w.T, router_b.reshape(N_EXPERTS, 1))

    n_rows = -(-(T * TOP_K + n_tiles * N_EXPERTS * (CHUNK - 1) + N_EXPERTS * (RB - CHUNK)) // RB) * RB
    n_blocks = n_rows // RB
    rows = tot[:, 0] * CHUNK
    padded = (rows + RB - 1) // RB * RB
    pend = jnp.cumsum(padded)
    pstart = pend - padded
    seg16, off16, before16 = tab[:, :, 0], tab[:, :, 1], tab[:, :, 2]
    dst16 = pstart[None, :] // CHUNK + before16
    tabs = jnp.concatenate([seg16, off16, dst16, jnp.zeros_like(seg16)], axis=1).reshape(n_tiles, 1, 4 * N_EXPERTS)
    fill_start = jnp.concatenate([pstart + rows, pend[-1:]]) // CHUNK
    fill_n = jnp.concatenate([padded - rows, n_rows - pend[-1:]]) // CHUNK
    blk_e = jnp.minimum(jnp.sum(pend[None, :] <= (jnp.arange(n_blocks, dtype=I32) * RB)[:, None], axis=1),
                        N_EXPERTS - 1).astype(I32)
    nused = (pend[-1:] // RB).astype(I32)

    tab_spec = pl.BlockSpec((1, 1, 4 * N_EXPERTS), lambda i, *_: (i, 0, 0), memory_space=pltpu.SMEM)
    xs = pl.pallas_call(
        _dispatch_kernel,
        grid_spec=pltpu.PrefetchScalarGridSpec(
            num_scalar_prefetch=2, grid=(n_tiles,),
            in_specs=[tab_spec, pl.BlockSpec((TM, D), lambda i, *_: (i, 0)),
                      pl.BlockSpec((TOP_K, TM), lambda i, *_: (0, i))],
            out_specs=pl.BlockSpec(memory_space=pl.ANY),
            scratch_shapes=[pltpu.VMEM((SORT_ROWS, D), BF16), pltpu.VMEM((CHUNK, D), BF16),
                            pltpu.SemaphoreType.DMA(())]),
        out_shape=jax.ShapeDtypeStruct((n_rows, D), BF16),
        compiler_params=_params(("arbitrary",)),
        name="dispatch",
    )(fill_start.astype(I32), fill_n.astype(I32), tabs, hn, lpos)

    last = lambda i, nu: jnp.minimum(i, nu[0] - 1)
    y = pl.pallas_call(
        _expert_kernel,
        grid_spec=pltpu.PrefetchScalarGridSpec(
            num_scalar_prefetch=2, grid=(n_blocks,),
            in_specs=[pl.BlockSpec((RB, D), lambda i, be, nu: (last(i, nu), 0)),
                      pl.BlockSpec((1, D, 2 * D_EXPERT), lambda i, be, nu: (be[last(i, nu)], 0, 0)),
                      pl.BlockSpec((1, 1, 2 * D_EXPERT), lambda i, be, nu: (be[last(i, nu)], 0, 0)),
                      pl.BlockSpec((1, D_EXPERT, D), lambda i, be, nu: (be[last(i, nu)], 0, 0)),
                      pl.BlockSpec((1, 1, D), lambda i, be, nu: (be[last(i, nu)], 0, 0))],
            out_specs=pl.BlockSpec((RB, D), lambda i, be, nu: (i, 0))),
        out_shape=jax.ShapeDtypeStruct((n_rows, D), BF16),
        compiler_params=_params(("arbitrary",)),
        name="experts",
    )(blk_e, nused, xs, w_gate_up.astype(BF16), b_gate_up.reshape(N_EXPERTS, 1, -1),
      w_down.astype(BF16), b_down.reshape(N_EXPERTS, 1, -1))

    out = pl.pallas_call(
        _final_kernel,
        grid=(n_tiles,),
        in_specs=[pl.BlockSpec((1, 1, 4 * N_EXPERTS), lambda i: (i, 0, 0), memory_space=pltpu.SMEM),
                  pl.BlockSpec(memory_space=pl.ANY),
                  pl.BlockSpec((TM, D), lambda i: (i, 0)), pl.BlockSpec((TM, TOP_K), lambda i: (i, 0)),
                  pl.BlockSpec((TM, TOP_K), lambda i: (i, 0)),
                  pl.BlockSpec((TM, PLE_DIM), lambda i: (i, 0)), full((1, D)), full((D, D)), full((PLE_DIM, D)),
                  full((1, D)), full((1, D))],
        out_specs=pl.BlockSpec((TM, D), lambda i: (i, 0)),
        out_shape=jax.ShapeDtypeStruct((T, D), F32),
        scratch_shapes=[pltpu.VMEM((SORT_ROWS, D), BF16), pltpu.SemaphoreType.DMA(())],
        compiler_params=_params(("arbitrary",)),
        name="final",
    )(tabs, y, x1, lpos.T, gates.T, p.reshape(T, PLE_DIM), row2(ple_norm), w_ple_gate.astype(BF16),
      w_ple_proj.astype(BF16), row2(ple_post_norm), row2(out_norm))
    return out.reshape(B, S, D)


def kernel(x, p, mix_norm, w_in, gmlp_ln_g, gmlp_ln_b, gmlp_ws, gmlp_bs, attn_out_norm, gmlp_out_norm, w_out, ffn_norm, router_w, router_b, w_gate_up, b_gate_up, w_down, b_down, ple_norm, w_ple_gate, w_ple_proj, ple_post_norm, final_norm):
    depth = p.shape[0]
    assert depth == 1, "the final rmsnorm is fused into the (single) layer"
    return _layer(x, p[0], mix_norm[0], w_in[0], gmlp_ln_g[0], gmlp_ln_b[0], gmlp_ws[0], gmlp_bs[0],
                  attn_out_norm[0], gmlp_out_norm[0], w_out[0], ffn_norm[0], router_w[0], router_b[0],
                  w_gate_up[0], b_gate_up[0], w_down[0], b_down[0], ple_norm[0], w_ple_gate[0],
                  w_ple_proj[0], ple_post_norm[0], final_norm)
```

```python
import jax
import jax.numpy as jnp
import numpy as np
from jax import lax
from jax.experimental import pallas as pl
from jax.experimental.pallas import tpu as pltpu

F32 = jnp.float32
BF16 = jnp.bfloat16
I32 = jnp.int32

D_MODEL = 1024
PLE_DIM = 256
ATTN_WIDTH = 512
HEAD_DIM = 64
ROPE_THETA = 10000.0
MOBA_BLOCK = 256
MOBA_TOPK = 3
GMLP_WIDTH = 512
GMLP_GROUPS = 4
GMLP_GROUP_DIM = 128
GMLP_CHUNK = 128
N_EXPERTS = 32
TOP_K = 4
D_EXPERT = 1024
SWIGLU_LIMIT = 7.0
SWIGLU_ALPHA = 1.702
NORM_EPS = 1e-6
NEG_INF = -1e30
Q_SCALE = float(np.log2(np.e) / np.sqrt(HEAD_DIM))

LANES = 128
CHUNK = 16

TM = 512
RB = 512
FC = 512
PAIRS = TM * TOP_K
SORT_ROWS = 2560
TAB_SEG, TAB_OFF, TAB_DST = 0, N_EXPERTS, 2 * N_EXPERTS
VMEM_LIMIT = 48 * 1024 * 1024
EXPERT_VMEM_LIMIT = 56 * 1024 * 1024

assert SORT_ROWS % TM == 0 and SORT_ROWS >= PAIRS + N_EXPERTS * (CHUNK - 1)


def _rms(x, g):
    return x * lax.rsqrt(jnp.mean(x * x, axis=-1, keepdims=True) + NORM_EPS) * g


def _nt(a, b):
    return lax.dot_general(a, b, (((1,), (1,)), ((), ())), preferred_element_type=F32)


def _proj_kernel(x_ref, mixn_ref, wk_ref, wqvt_ref, wuv_ref, cos_ref, sin_ref, cost_ref, sint_ref, lng_ref,
                 lnb_ref, ws_ref, bs_ref, gon_ref, qt_ref, k_ref, vt_ref, kmean_ref, gm_ref, gacc_ref):
    h = _rms(x_ref[...], mixn_ref[...]).astype(BF16)
    half = HEAD_DIM // 2

    kx = jnp.dot(h, wk_ref[...], preferred_element_type=F32)
    lane = lax.broadcasted_iota(I32, (TM, LANES), 1)
    first_half = (lane % HEAD_DIM) < half
    parts = []
    for s in range(ATTN_WIDTH // LANES):
        slab = kx[:, s * LANES:(s + 1) * LANES]
        ahead = pltpu.roll(slab, LANES - half, 1)
        behind = pltpu.roll(slab, half, 1)
        parts.append(jnp.where(first_half, ahead, behind))
    k = kx * cos_ref[...] + jnp.concatenate(parts, axis=1) * sin_ref[...]
    k_ref[...] = k.astype(BF16)
    for c in range(TM // MOBA_BLOCK):
        kmean_ref[0, c:c + 1, :] = jnp.mean(k[c * MOBA_BLOCK:(c + 1) * MOBA_BLOCK], axis=0, keepdims=True)

    qvt = _nt(wqvt_ref[...], h)
    qx = qvt[:ATTN_WIDTH]
    swapped = []
    for hd in range(ATTN_WIDTH // HEAD_DIM):
        r0 = hd * HEAD_DIM
        swapped += [qx[r0 + half:r0 + HEAD_DIM], qx[r0:r0 + half]]
    qt = (qx * cost_ref[...] + jnp.concatenate(swapped, axis=0) * sint_ref[...]) * Q_SCALE
    qt_ref[...] = qt.astype(BF16)
    vt_ref[...] = qvt[ATTN_WIDTH:].astype(BF16)

    uv = jnp.dot(h, wuv_ref[...], preferred_element_type=F32)
    inv_sqrt2 = np.float32(1.0 / np.sqrt(2.0))

    def gelu(t):
        return 0.5 * t * (1.0 + lax.erf(t * inv_sqrt2))

    gu = gelu(uv[:, :GMLP_WIDTH])
    gv = gelu(uv[:, GMLP_WIDTH:])
    row = lax.broadcasted_iota(I32, (GMLP_CHUNK, GMLP_CHUNK), 0)
    col = lax.broadcasted_iota(I32, (GMLP_CHUNK, GMLP_CHUNK), 1)
    tril = col <= row
    for g in range(GMLP_GROUPS):
        sl = slice(g * GMLP_GROUP_DIM, (g + 1) * GMLP_GROUP_DIM)
        vg = gv[:, sl]
        mu = jnp.mean(vg, axis=-1, keepdims=True)
        dv = vg - mu
        var = jnp.mean(dv * dv, axis=-1, keepdims=True)
        vn = (dv * lax.rsqrt(var + NORM_EPS) * lng_ref[:, sl] + lnb_ref[:, sl]).astype(BF16)
        w = jnp.where(tril, ws_ref[g], 0.0).astype(BF16)
        bias = bs_ref[g]
        for c in range(TM // GMLP_CHUNK):
            rs = slice(c * GMLP_CHUNK, (c + 1) * GMLP_CHUNK)
            mixed = jnp.dot(w, vn[rs], preferred_element_type=F32) + bias
            gacc_ref[rs, sl] = gu[rs, sl] * mixed
    gm_ref[...] = _rms(gacc_ref[...], gon_ref[...]).astype(BF16)


def _attn_kernel(qt_ref, k_ref, vt_ref, kmean_ref, o_ref):
    seq = k_ref.shape[0]
    n_blk = seq // MOBA_BLOCK
    feat = lax.broadcasted_iota(I32, (LANES, 1), 0)
    lane = lax.broadcasted_iota(I32, (1, LANES), 1)
    blk_of_key = lax.broadcasted_iota(I32, (n_blk, seq), 0)
    blk_of_query = lax.broadcasted_iota(I32, (n_blk, seq), 1) // MOBA_BLOCK
    key_pos = lax.broadcasted_iota(I32, (MOBA_BLOCK, MOBA_BLOCK), 0)
    query_pos = lax.broadcasted_iota(I32, (MOBA_BLOCK, MOBA_BLOCK), 1)
    causal = key_pos <= query_pos
    blk = lambda i: slice(i * MOBA_BLOCK, (i + 1) * MOBA_BLOCK)

    for h in range(LANES // HEAD_DIM):
        hs = slice(h * HEAD_DIM, (h + 1) * HEAD_DIM)
        rows_in_head = (feat >= h * HEAD_DIM) & (feat < (h + 1) * HEAD_DIM)
        qt = jnp.where(rows_in_head, qt_ref[...], jnp.zeros((), BF16))

        km = jnp.where((lane >= h * HEAD_DIM) & (lane < (h + 1) * HEAD_DIM), kmean_ref[0], 0.0)
        km1 = km.astype(BF16)
        r1 = km - km1.astype(F32)
        km2 = r1.astype(BF16)
        km3 = (r1 - km2.astype(F32)).astype(BF16)
        dotf = lambda a, b: jnp.dot(a, b, preferred_element_type=F32)
        gate = dotf(km1, qt) + dotf(km2, qt) + dotf(km3, qt)
        rank = jnp.zeros((n_blk, seq), F32)
        for m in range(n_blk):
            gm_ = gate[m:m + 1, :]
            beats = (m < blk_of_query) & ((gm_ > gate) | ((gm_ == gate) & (m < blk_of_key)))
            rank = rank + jnp.where(beats, 1.0, 0.0)
        chosen = (blk_of_key < blk_of_query) & (rank < MOBA_TOPK)
        bias = jnp.where(chosen, 0.0, NEG_INF)

        for i in range(n_blk):
            n_keys = (i + 1) * MOBA_BLOCK
            s = dotf(k_ref[0:n_keys, :], qt[:, blk(i)])
            tiles = [s[blk(j)] + bias[j:j + 1, blk(i)] for j in range(i)]
            tiles.append(jnp.where(causal, s[blk(i)], NEG_INF))
            top = tiles[0]
            for t in tiles[1:]:
                top = jnp.maximum(top, t)
            m_q = jnp.max(top, axis=0, keepdims=True)
            probs = [jnp.exp2(t - m_q) for t in tiles]
            total = probs[0]
            for pj in probs[1:]:
                total = total + pj
            inv = 1.0 / jnp.sum(total, axis=0, keepdims=True)
            pu = jnp.concatenate([pj.astype(BF16) for pj in probs], axis=0)
            o_ref[i, hs, :] = dotf(vt_ref[hs, 0:n_keys], pu) * inv


def _mix_kernel(x_ref, at_ref, gm_ref, aon_ref, woa_ref, wog_ref, ffn_ref, rwt_ref, rb_ref,
                x1_ref, hn_ref, lpos_ref, gate_ref, tab_ref, tot_ref, carry_ref):
    @pl.when(pl.program_id(0) == 0)
    def _():
        carry_ref[...] = jnp.zeros_like(carry_ref)

    parts = []
    for c in range(TM // MOBA_BLOCK):
        at = at_ref[c]
        ss = jnp.sum(at * at, axis=0, keepdims=True)
        atn = at * lax.rsqrt(ss * (1.0 / ATTN_WIDTH) + NORM_EPS) * aon_ref[...]
        parts.append(atn.T)
    attn = jnp.concatenate(parts, axis=0).astype(BF16)
    x1 = (x_ref[...] + jnp.dot(attn, woa_ref[...], preferred_element_type=F32)
          + jnp.dot(gm_ref[...], wog_ref[...], preferred_element_type=F32))
    x1_ref[...] = x1
    hn = _rms(x1, ffn_ref[...])
    hn_ref[...] = hn.astype(BF16)

    logits = lax.dot_general(rwt_ref[...], hn, (((1,), (1,)), ((), ())),
                             precision=lax.Precision.HIGHEST, preferred_element_type=F32) + rb_ref[...]
    eidx = lax.broadcasted_iota(I32, (N_EXPERTS, TM), 0)
    vals, idxs = [], []
    rest = logits
    for _ in range(TOP_K):
        m = jnp.max(rest, axis=0, keepdims=True)
        idx = jnp.min(jnp.where(rest == m, eidx, N_EXPERTS), axis=0, keepdims=True)
        vals.append(m)
        idxs.append(idx)
        rest = jnp.where(eidx == idx, -jnp.inf, rest)
    exps = [jnp.exp(v - vals[0]) for v in vals]
    denom = exps[0] + exps[1] + exps[2] + exps[3]
    hot = [eidx == idx for idx in idxs]
    multi = jnp.where(hot[0] | hot[1] | hot[2] | hot[3], 1.0, 0.0)

    tp = lax.broadcasted_iota(I32, (TM, TM), 0)
    tq = lax.broadcasted_iota(I32, (TM, TM), 1)
    earlier = jnp.where(tp < tq, 1.0, 0.0).astype(BF16)
    before = jnp.dot(multi.astype(BF16), earlier, preferred_element_type=F32)
    count = jnp.sum(multi, axis=1, keepdims=True)
    seg = jnp.floor((count + (CHUNK - 1)) * (1.0 / CHUNK))
    ep = lax.broadcasted_iota(I32, (N_EXPERTS, N_EXPERTS), 0)
    eq = lax.broadcasted_iota(I32, (N_EXPERTS, N_EXPERTS), 1)
    lower = jnp.where(eq < ep, 1.0, 0.0).astype(BF16)
    seg_wide = jnp.broadcast_to(seg, (N_EXPERTS, LANES))
    off = jnp.dot(lower, seg_wide.astype(BF16), preferred_element_type=F32)
    slot = off[:, 0:1] * CHUNK + before
    for kk in range(TOP_K):
        lpos_ref[kk:kk + 1, :] = jnp.sum(jnp.where(hot[kk], slot, 0.0), axis=0, keepdims=True).astype(I32)
        gate_ref[kk:kk + 1, :] = exps[kk] / denom
    lane = lax.broadcasted_iota(I32, (N_EXPERTS, LANES), 1)
    tab = jnp.where(lane == 0, seg_wide, jnp.where(lane == 1, off, carry_ref[...]))
    tab_ref[0] = tab.astype(I32)
    carry_ref[...] = carry_ref[...] + seg
    tot_ref[...] = carry_ref[...].astype(I32)


def _chunk_rows(ref, chunk_index):
    return ref.at[pl.ds(pl.multiple_of(chunk_index * CHUNK, CHUNK), CHUNK), :]


def _copy_segments(tab_ref, sorted_ref, rows_ref, sem, to_rows):
    def per_expert(e, started):
        n = tab_ref[0, 0, TAB_SEG + e]
        off = tab_ref[0, 0, TAB_OFF + e]
        dst = tab_ref[0, 0, TAB_DST + e]

        def per_chunk(c, _):
            a, b = _chunk_rows(sorted_ref, off + c), _chunk_rows(rows_ref, dst + c)
            (pltpu.make_async_copy(a, b, sem) if to_rows else pltpu.make_async_copy(b, a, sem)).start()
            return 0

        lax.fori_loop(0, n, per_chunk, 0)
        return started + n

    return lax.fori_loop(0, N_EXPERTS, per_expert, 0)


def _wait_chunks(n, a_ref, b_ref, sem):
    def wait(_, carry):
        pltpu.make_async_copy(_chunk_rows(a_ref, 0), _chunk_rows(b_ref, 0), sem).wait()
        return carry

    lax.fori_loop(0, n, wait, 0)


def _dispatch_kernel(fill_start_ref, fill_n_ref, tab_ref, hn_ref, lpos_ref, xs_ref, sorted_ref, zero_ref, sems,
                     started_ref):
    i = pl.program_id(0)
    slot = i % 2
    buf = sorted_ref.at[slot]

    @pl.when(i >= 2)
    def _():
        _wait_chunks(started_ref[slot], buf, xs_ref, sems.at[slot])

    hn = hn_ref[...]
    lpos = lpos_ref[...]
    for c in range(SORT_ROWS // TM):
        r = lax.broadcasted_iota(I32, (TM, TM), 0) + c * TM
        hit = (r == lpos[0:1]) | (r == lpos[1:2]) | (r == lpos[2:3]) | (r == lpos[3:4])
        onehot = jnp.where(hit, 1.0, 0.0).astype(BF16)
        buf[c * TM:(c + 1) * TM, :] = jnp.dot(onehot, hn, preferred_element_type=F32).astype(BF16)
    started_ref[slot] = _copy_segments(tab_ref, buf, xs_ref, sems.at[slot], to_rows=True)

    @pl.when(i == pl.num_programs(0) - 1)
    def _():
        @pl.when(i >= 1)
        def _():
            _wait_chunks(started_ref[1 - slot], sorted_ref.at[1 - slot], xs_ref, sems.at[1 - slot])

        _wait_chunks(started_ref[slot], buf, xs_ref, sems.at[slot])
        zero_ref[...] = jnp.zeros_like(zero_ref)

        def per_range(e, started):
            n = fill_n_ref[e]
            base = fill_start_ref[e]

            def per_chunk(c, _):
                pltpu.make_async_copy(zero_ref, _chunk_rows(xs_ref, base + c), sems.at[0]).start()
                return 0

            lax.fori_loop(0, n, per_chunk, 0)
            return started + n

        n_fill = lax.fori_loop(0, N_EXPERTS + 1, per_range, 0)
        _wait_chunks(n_fill, zero_ref, xs_ref, sems.at[0])


def _expert_kernel(blke_ref, nused_ref, xs_ref, wgu_ref, bgu_ref, wdn_ref, bdn_ref, y_ref, wgu_bf, wdn_bf):
    i = pl.program_id(0)
    used = i < nused_ref[0]

    @pl.when(used & ((i == 0) | (blke_ref[i] != blke_ref[jnp.maximum(i - 1, 0)])))
    def _():
        for c in range(2 * D_EXPERT // FC):
            wgu_bf[:, c * FC:(c + 1) * FC] = wgu_ref[0, :, c * FC:(c + 1) * FC].astype(BF16)
        for c in range(D_EXPERT // FC):
            wdn_bf[c * FC:(c + 1) * FC, :] = wdn_ref[0, c * FC:(c + 1) * FC, :].astype(BF16)

    @pl.when(used)
    def _():
        x = xs_ref[...]
        y = jnp.zeros((RB, D_MODEL), F32) + bdn_ref[0]
        for c in range(D_EXPERT // FC):
            g = jnp.dot(x, wgu_bf[:, c * FC:(c + 1) * FC], preferred_element_type=F32) \
                + bgu_ref[0, :, c * FC:(c + 1) * FC]
            lin = jnp.dot(x, wgu_bf[:, D_EXPERT + c * FC:D_EXPERT + (c + 1) * FC],
                          preferred_element_type=F32) + bgu_ref[0, :, D_EXPERT + c * FC:D_EXPERT + (c + 1) * FC]
            g = jnp.minimum(g, SWIGLU_LIMIT)
            lin = jnp.clip(lin, -SWIGLU_LIMIT, SWIGLU_LIMIT)
            a = g * (1.0 / (1.0 + jnp.exp(-SWIGLU_ALPHA * g))) * (lin + 1.0)
            y = y + jnp.dot(a.astype(BF16), wdn_bf[c * FC:(c + 1) * FC, :], preferred_element_type=F32)
        y_ref[...] = y.astype(BF16)

    @pl.when(i >= nused_ref[0])
    def _():
        y_ref[...] = jnp.zeros_like(y_ref)


def _final_kernel(tab_ref, tab_next_ref, y_ref, x1_ref, lpos_ref, gate_ref, p_ref, plen_ref, wg_ref, wp_ref,
                  post_ref, fin_ref, o_ref, sorted_ref, sems):
    i = pl.program_id(0)
    slot = i % 2

    def fetch(tab, s):
        buf = sorted_ref.at[s]
        buf[PAIRS:, :] = jnp.zeros((SORT_ROWS - PAIRS, D_MODEL), BF16)
        _copy_segments(tab, buf, y_ref, sems.at[s], to_rows=False)

    @pl.when(i == 0)
    def _():
        fetch(tab_ref, 0)

    @pl.when(i + 1 < pl.num_programs(0))
    def _():
        fetch(tab_next_ref, 1 - slot)

    buf = sorted_ref.at[slot]
    n_chunks = lax.fori_loop(0, N_EXPERTS, lambda e, n: n + tab_ref[0, 0, TAB_SEG + e], 0)
    _wait_chunks(n_chunks, y_ref, buf, sems.at[slot])

    lpos = lpos_ref[...]
    gates = gate_ref[...]
    moe = jnp.zeros((TM, D_MODEL), F32)
    for c in range(SORT_ROWS // TM):
        r = lax.broadcasted_iota(I32, (TM, TM), 1) + c * TM
        w = jnp.zeros((TM, TM), F32)
        for kk in range(TOP_K):
            w = w + jnp.where(r == lpos[:, kk:kk + 1], gates[:, kk:kk + 1], 0.0)
        moe = moe + jnp.dot(w.astype(BF16), buf[c * TM:(c + 1) * TM, :], preferred_element_type=F32)
    x2 = x1_ref[...] + moe
    z = jnp.dot(_rms(x2, plen_ref[...]).astype(BF16), wg_ref[...], preferred_element_type=F32)
    ple_gate = 1.0 / (1.0 + jnp.exp(-z))
    emb = jnp.dot(p_ref[...].astype(BF16), wp_ref[...], preferred_element_type=F32)
    x3 = x2 + ple_gate * _rms(emb, post_ref[...])
    o_ref[...] = _rms(x3, fin_ref[...])


def _params(sem):
    return pltpu.CompilerParams(dimension_semantics=sem, vmem_limit_bytes=VMEM_LIMIT)


def _rope_tables(seq):
    half = HEAD_DIM // 2
    inv = ROPE_THETA ** (-jnp.arange(half, dtype=F32) / half)
    ang = jnp.arange(seq, dtype=F32)[:, None] * inv[None, :]
    cos = jnp.cos(ang)
    sin = jnp.sin(ang)
    n_heads = ATTN_WIDTH // HEAD_DIM
    cos_t = jnp.tile(jnp.concatenate([cos, cos], axis=1), (1, n_heads))
    sin_t = jnp.tile(jnp.concatenate([-sin, sin], axis=1), (1, n_heads))
    return cos_t, sin_t


def _layer(x, p, mix_norm, w_in, gmlp_ln_g, gmlp_ln_b, gmlp_ws, gmlp_bs, attn_out_norm, gmlp_out_norm,
           w_out, ffn_norm, router_w, router_b, w_gate_up, b_gate_up, w_down, b_down, ple_norm,
           w_ple_gate, w_ple_proj, ple_post_norm, out_norm):
    B, S, D = x.shape
    T = B * S
    A = ATTN_WIDTH
    n_tiles = T // TM
    n_blk = S // MOBA_BLOCK
    tiles_per_seq = S // TM
    xf = x.reshape(T, D)
    row2 = lambda v: v.reshape(1, -1)
    full = lambda shape: pl.BlockSpec(shape, lambda *_: (0,) * len(shape))

    cos_t, sin_t = _rope_tables(S)
    wk = w_in[:, A:2 * A].astype(BF16)
    wqvt = jnp.concatenate([w_in[:, :A], w_in[:, 2 * A:3 * A]], axis=1).T.astype(BF16)
    wuv = w_in[:, 3 * A:].astype(BF16)

    qt, k, vt, kmean, gm = pl.pallas_call(
        _proj_kernel,
        grid=(n_tiles,),
        in_specs=[pl.BlockSpec((TM, D), lambda i: (i, 0)), full((1, D)), full((D, A)), full((2 * A, D)),
                  full((D, 2 * GMLP_WIDTH)),
                  pl.BlockSpec((TM, A), lambda i: (i % tiles_per_seq, 0)),
                  pl.BlockSpec((TM, A), lambda i: (i % tiles_per_seq, 0)),
                  pl.BlockSpec((A, TM), lambda i: (0, i % tiles_per_seq)),
                  pl.BlockSpec((A, TM), lambda i: (0, i % tiles_per_seq)),
                  full((1, GMLP_WIDTH)), full((1, GMLP_WIDTH)),
                  full((GMLP_GROUPS, GMLP_CHUNK, GMLP_CHUNK)), full((GMLP_GROUPS, GMLP_CHUNK, 1)),
                  full((1, GMLP_WIDTH))],
        out_specs=[pl.BlockSpec((A, TM), lambda i: (0, i)), pl.BlockSpec((TM, A), lambda i: (i, 0)),
                   pl.BlockSpec((A, TM), lambda i: (0, i)),
                   pl.BlockSpec((1, TM // MOBA_BLOCK, A), lambda i: (i, 0, 0)),
                   pl.BlockSpec((TM, GMLP_WIDTH), lambda i: (i, 0))],
        out_shape=[jax.ShapeDtypeStruct((A, T), BF16), jax.ShapeDtypeStruct((T, A), BF16),
                   jax.ShapeDtypeStruct((A, T), BF16),
                   jax.ShapeDtypeStruct((n_tiles, TM // MOBA_BLOCK, A), F32),
                   jax.ShapeDtypeStruct((T, GMLP_WIDTH), BF16)],
        scratch_shapes=[pltpu.VMEM((TM, GMLP_WIDTH), F32)],
        compiler_params=_params(("parallel",)),
        name="proj",
    )(xf, row2(mix_norm), wk, wqvt, wuv, cos_t, sin_t, cos_t.T, sin_t.T, row2(gmlp_ln_g), row2(gmlp_ln_b), gmlp_ws,
      gmlp_bs.reshape(GMLP_GROUPS, GMLP_CHUNK, 1), row2(gmlp_out_norm))

    kmean = kmean.reshape(B, n_blk, A)
    n_pairs = A // LANES
    attn_t = pl.pallas_call(
        _attn_kernel,
        grid=(B, n_pairs),
        in_specs=[pl.BlockSpec((LANES, S), lambda b, h: (h, b)), pl.BlockSpec((S, LANES), lambda b, h: (b, h)),
                  pl.BlockSpec((LANES, S), lambda b, h: (h, b)),
                  pl.BlockSpec((1, n_blk, LANES), lambda b, h: (b, 0, h))],
        out_specs=pl.BlockSpec((n_blk, LANES, MOBA_BLOCK), lambda b, h: (b, h, 0)),
        out_shape=jax.ShapeDtypeStruct((T // MOBA_BLOCK, A, MOBA_BLOCK), F32),
        compiler_params=_params(("parallel", "parallel")),
        name="attn",
    )(qt, k, vt, kmean)

    x1, hn, lpos, gates, tab, tot = pl.pallas_call(
        _mix_kernel,
        grid=(n_tiles,),
        in_specs=[pl.BlockSpec((TM, D), lambda i: (i, 0)),
                  pl.BlockSpec((TM // MOBA_BLOCK, A, MOBA_BLOCK), lambda i: (i, 0, 0)),
                  pl.BlockSpec((TM, GMLP_WIDTH), lambda i: (i, 0)),
                  full((A, 1)), full((A, D)), full((GMLP_WIDTH, D)), full((1, D)),
                  full((N_EXPERTS, D)), full((N_EXPERTS, 1))],
        out_specs=[pl.BlockSpec((TM, D), lambda i: (i, 0)), pl.BlockSpec((TM, D), lambda i: (i, 0)),
                   pl.BlockSpec((TOP_K, TM), lambda i: (0, i)), pl.BlockSpec((TOP_K, TM), lambda i: (0, i)),
                   pl.BlockSpec((1, N_EXPERTS, LANES), lambda i: (i, 0, 0)), full((N_EXPERTS, LANES))],
        out_shape=[jax.ShapeDtypeStruct((T, D), F32), jax.ShapeDtypeStruct((T, D), BF16),
                   jax.ShapeDtypeStruct((TOP_K, T), I32), jax.ShapeDtypeStruct((TOP_K, T), F32),
                   jax.ShapeDtypeStruct((n_tiles, N_EXPERTS, LANES), I32),
                   jax.ShapeDtypeStruct((N_EXPERTS, LANES), I32)],
        scratch_shapes=[pltpu.VMEM((N_EXPERTS, LANES), F32)],
        compiler_params=_params(("arbitrary",)),
        name="mix",
    )(xf, attn_t, gm, attn_out_norm.reshape(A, 1), w_out[:A].astype(BF16), w_out[A:].astype(BF16),
      row2(ffn_norm), router_w.T, router_b.reshape(N_EXPERTS, 1))

    n_rows = -(-(T * TOP_K + n_tiles * N_EXPERTS * (CHUNK - 1) + N_EXPERTS * (RB - CHUNK)) // RB) * RB
    n_blocks = n_rows // RB
    rows = tot[:, 0] * CHUNK
    padded = (rows + RB - 1) // RB * RB
    pend = jnp.cumsum(padded)
    pstart = pend - padded
    seg16, off16, before16 = tab[:, :, 0], tab[:, :, 1], tab[:, :, 2]
    dst16 = pstart[None, :] // CHUNK + before16
    tabs = jnp.concatenate([seg16, off16, dst16, jnp.zeros_like(seg16)], axis=1).reshape(n_tiles, 1, 4 * N_EXPERTS)
    fill_start = jnp.concatenate([pstart + rows, pend[-1:]]) // CHUNK
    fill_n = jnp.concatenate([padded - rows, n_rows - pend[-1:]]) // CHUNK
    blk_e = jnp.minimum(jnp.sum(pend[None, :] <= (jnp.arange(n_blocks, dtype=I32) * RB)[:, None], axis=1),
                        N_EXPERTS - 1).astype(I32)
    nused = (pend[-1:] // RB).astype(I32)

    tab_spec = pl.BlockSpec((1, 1, 4 * N_EXPERTS), lambda i, *_: (i, 0, 0), memory_space=pltpu.SMEM)
    xs = pl.pallas_call(
        _dispatch_kernel,
        grid_spec=pltpu.PrefetchScalarGridSpec(
            num_scalar_prefetch=2, grid=(n_tiles,),
            in_specs=[tab_spec, pl.BlockSpec((TM, D), lambda i, *_: (i, 0)),
                      pl.BlockSpec((TOP_K, TM), lambda i, *_: (0, i))],
            out_specs=pl.BlockSpec(memory_space=pl.ANY),
            scratch_shapes=[pltpu.VMEM((2, SORT_ROWS, D), BF16), pltpu.VMEM((CHUNK, D), BF16),
                            pltpu.SemaphoreType.DMA((2,)), pltpu.SMEM((2,), I32)]),
        out_shape=jax.ShapeDtypeStruct((n_rows, D), BF16),
        compiler_params=_params(("arbitrary",)),
        name="dispatch",
    )(fill_start.astype(I32), fill_n.astype(I32), tabs, hn, lpos)

    last = lambda i, nu: jnp.minimum(i, nu[0] - 1)
    y = pl.pallas_call(
        _expert_kernel,
        grid_spec=pltpu.PrefetchScalarGridSpec(
            num_scalar_prefetch=2, grid=(n_blocks,),
            in_specs=[pl.BlockSpec((RB, D), lambda i, be, nu: (last(i, nu), 0)),
                      pl.BlockSpec((1, D, 2 * D_EXPERT), lambda i, be, nu: (be[last(i, nu)], 0, 0)),
                      pl.BlockSpec((1, 1, 2 * D_EXPERT), lambda i, be, nu: (be[last(i, nu)], 0, 0)),
                      pl.BlockSpec((1, D_EXPERT, D), lambda i, be, nu: (be[last(i, nu)], 0, 0)),
                      pl.BlockSpec((1, 1, D), lambda i, be, nu: (be[last(i, nu)], 0, 0))],
            out_specs=pl.BlockSpec((RB, D), lambda i, be, nu: (i, 0)),
            scratch_shapes=[pltpu.VMEM((D, 2 * D_EXPERT), BF16), pltpu.VMEM((D_EXPERT, D), BF16)]),
        out_shape=jax.ShapeDtypeStruct((n_rows, D), BF16),
        compiler_params=pltpu.CompilerParams(dimension_semantics=("arbitrary",),
                                             vmem_limit_bytes=EXPERT_VMEM_LIMIT),
        name="experts",
    )(blk_e, nused, xs, w_gate_up, b_gate_up.reshape(N_EXPERTS, 1, -1), w_down, b_down.reshape(N_EXPERTS, 1, -1))

    out = pl.pallas_call(
        _final_kernel,
        grid=(n_tiles,),
        in_specs=[pl.BlockSpec((1, 1, 4 * N_EXPERTS), lambda i: (i, 0, 0), memory_space=pltpu.SMEM),
                  pl.BlockSpec((1, 1, 4 * N_EXPERTS), lambda i: (jnp.minimum(i + 1, n_tiles - 1), 0, 0),
                               memory_space=pltpu.SMEM),
                  pl.BlockSpec(memory_space=pl.ANY),
                  pl.BlockSpec((TM, D), lambda i: (i, 0)), pl.BlockSpec((TM, TOP_K), lambda i: (i, 0)),
                  pl.BlockSpec((TM, TOP_K), lambda i: (i, 0)),
                  pl.BlockSpec((TM, PLE_DIM), lambda i: (i, 0)), full((1, D)), full((D, D)), full((PLE_DIM, D)),
                  full((1, D)), full((1, D))],
        out_specs=pl.BlockSpec((TM, D), lambda i: (i, 0)),
        out_shape=jax.ShapeDtypeStruct((T, D), F32),
        scratch_shapes=[pltpu.VMEM((2, SORT_ROWS, D), BF16), pltpu.SemaphoreType.DMA((2,))],
        compiler_params=_params(("arbitrary",)),
        name="final",
    )(tabs, tabs, y, x1, lpos.T, gates.T, p.reshape(T, PLE_DIM), row2(ple_norm), w_ple_gate.astype(BF16),
      w_ple_proj.astype(BF16), row2(ple_post_norm), row2(out_norm))
    return out.reshape(B, S, D)


def kernel(x, p, mix_norm, w_in, gmlp_ln_g, gmlp_ln_b, gmlp_ws, gmlp_bs, attn_out_norm, gmlp_out_norm, w_out, ffn_norm, router_w, router_b, w_gate_up, b_gate_up, w_down, b_down, ple_norm, w_ple_gate, w_ple_proj, ple_post_norm, final_norm):
    depth = p.shape[0]
    assert depth == 1, "the final rmsnorm is fused into the (single) layer"
    return _layer(x, p[0], mix_norm[0], w_in[0], gmlp_ln_g[0], gmlp_ln_b[0], gmlp_ws[0], gmlp_bs[0],
                  attn_out_norm[0], gmlp_out_norm[0], w_out[0], ffn_norm[0], router_w[0], router_b[0],
                  w_gate_up[0], b_gate_up[0], w_down[0], b_down[0], ple_norm[0], w_ple_gate[0],
                  w_ple_proj[0], ple_post_norm[0], final_norm)
```

```python
import jax
import jax.numpy as jnp
import numpy as np
from jax import lax
from jax.experimental import pallas as pl
from jax.experimental.pallas import tpu as pltpu

F32 = jnp.float32
BF16 = jnp.bfloat16
I32 = jnp.int32

D_MODEL = 1024
PLE_DIM = 256
ATTN_WIDTH = 512
HEAD_DIM = 64
ROPE_THETA = 10000.0
MOBA_BLOCK = 256
MOBA_TOPK = 3
GMLP_WIDTH = 512
GMLP_GROUPS = 4
GMLP_GROUP_DIM = 128
GMLP_CHUNK = 128
N_EXPERTS = 32
TOP_K = 4
D_EXPERT = 1024
SWIGLU_LIMIT = 7.0
SWIGLU_ALPHA = 1.702
NORM_EPS = 1e-6
NEG_INF = -1e30
Q_SCALE = float(np.log2(np.e) / np.sqrt(HEAD_DIM))

LANES = 128
CHUNK = 16

TM = 512
RB = 512
FC = 512
PAIRS = TM * TOP_K
SORT_ROWS = 2560
CHUNKS_PER_BLOCK = RB // CHUNK
CHUNKS_PER_TILE = SORT_ROWS // CHUNK
DUMP_ROWS = 2 * RB
VMEM_LIMIT = 48 * 1024 * 1024
EXPERT_VMEM_LIMIT = 56 * 1024 * 1024

assert SORT_ROWS % TM == 0 and SORT_ROWS >= PAIRS + N_EXPERTS * (CHUNK - 1)


def _rms(x, g):
    return x * lax.rsqrt(jnp.mean(x * x, axis=-1, keepdims=True) + NORM_EPS) * g


def _nt(a, b):
    return lax.dot_general(a, b, (((1,), (1,)), ((), ())), preferred_element_type=F32)


def _proj_kernel(x_ref, mixn_ref, wk_ref, wqvt_ref, wuv_ref, cos_ref, sin_ref, cost_ref, sint_ref, lng_ref,
                 lnb_ref, ws_ref, bs_ref, gon_ref, qt_ref, k_ref, vt_ref, kmean_ref, gm_ref, gacc_ref):
    h = _rms(x_ref[...], mixn_ref[...]).astype(BF16)
    half = HEAD_DIM // 2

    kx = jnp.dot(h, wk_ref[...], preferred_element_type=F32)
    lane = lax.broadcasted_iota(I32, (TM, LANES), 1)
    first_half = (lane % HEAD_DIM) < half
    parts = []
    for s in range(ATTN_WIDTH // LANES):
        slab = kx[:, s * LANES:(s + 1) * LANES]
        ahead = pltpu.roll(slab, LANES - half, 1)
        behind = pltpu.roll(slab, half, 1)
        parts.append(jnp.where(first_half, ahead, behind))
    k = kx * cos_ref[...] + jnp.concatenate(parts, axis=1) * sin_ref[...]
    k_ref[...] = k.astype(BF16)
    for c in range(TM // MOBA_BLOCK):
        kmean_ref[0, c:c + 1, :] = jnp.mean(k[c * MOBA_BLOCK:(c + 1) * MOBA_BLOCK], axis=0, keepdims=True)

    qvt = _nt(wqvt_ref[...], h)
    qx = qvt[:ATTN_WIDTH]
    swapped = []
    for hd in range(ATTN_WIDTH // HEAD_DIM):
        r0 = hd * HEAD_DIM
        swapped += [qx[r0 + half:r0 + HEAD_DIM], qx[r0:r0 + half]]
    qt = (qx * cost_ref[...] + jnp.concatenate(swapped, axis=0) * sint_ref[...]) * Q_SCALE
    qt_ref[...] = qt.astype(BF16)
    vt_ref[...] = qvt[ATTN_WIDTH:].astype(BF16)

    uv = jnp.dot(h, wuv_ref[...], preferred_element_type=F32)
    inv_sqrt2 = np.float32(1.0 / np.sqrt(2.0))

    def gelu(t):
        return 0.5 * t * (1.0 + lax.erf(t * inv_sqrt2))

    gu = gelu(uv[:, :GMLP_WIDTH])
    gv = gelu(uv[:, GMLP_WIDTH:])
    row = lax.broadcasted_iota(I32, (GMLP_CHUNK, GMLP_CHUNK), 0)
    col = lax.broadcasted_iota(I32, (GMLP_CHUNK, GMLP_CHUNK), 1)
    tril = col <= row
    for g in range(GMLP_GROUPS):
        sl = slice(g * GMLP_GROUP_DIM, (g + 1) * GMLP_GROUP_DIM)
        vg = gv[:, sl]
        mu = jnp.mean(vg, axis=-1, keepdims=True)
        dv = vg - mu
        var = jnp.mean(dv * dv, axis=-1, keepdims=True)
        vn = (dv * lax.rsqrt(var + NORM_EPS) * lng_ref[:, sl] + lnb_ref[:, sl]).astype(BF16)
        w = jnp.where(tril, ws_ref[g], 0.0).astype(BF16)
        bias = bs_ref[g]
        for c in range(TM // GMLP_CHUNK):
            rs = slice(c * GMLP_CHUNK, (c + 1) * GMLP_CHUNK)
            mixed = jnp.dot(w, vn[rs], preferred_element_type=F32) + bias
            gacc_ref[rs, sl] = gu[rs, sl] * mixed
    gm_ref[...] = _rms(gacc_ref[...], gon_ref[...]).astype(BF16)


def _attn_kernel(qt_ref, k_ref, vt_ref, kmean_ref, o_ref):
    seq = k_ref.shape[0]
    n_blk = seq // MOBA_BLOCK
    feat = lax.broadcasted_iota(I32, (LANES, 1), 0)
    lane = lax.broadcasted_iota(I32, (1, LANES), 1)
    blk_of_key = lax.broadcasted_iota(I32, (n_blk, seq), 0)
    blk_of_query = lax.broadcasted_iota(I32, (n_blk, seq), 1) // MOBA_BLOCK
    key_pos = lax.broadcasted_iota(I32, (MOBA_BLOCK, MOBA_BLOCK), 0)
    query_pos = lax.broadcasted_iota(I32, (MOBA_BLOCK, MOBA_BLOCK), 1)
    causal = key_pos <= query_pos
    blk = lambda i: slice(i * MOBA_BLOCK, (i + 1) * MOBA_BLOCK)

    for h in range(LANES // HEAD_DIM):
        hs = slice(h * HEAD_DIM, (h + 1) * HEAD_DIM)
        rows_in_head = (feat >= h * HEAD_DIM) & (feat < (h + 1) * HEAD_DIM)
        qt = jnp.where(rows_in_head, qt_ref[...], jnp.zeros((), BF16))

        km = jnp.where((lane >= h * HEAD_DIM) & (lane < (h + 1) * HEAD_DIM), kmean_ref[0], 0.0)
        km1 = km.astype(BF16)
        r1 = km - km1.astype(F32)
        km2 = r1.astype(BF16)
        km3 = (r1 - km2.astype(F32)).astype(BF16)
        dotf = lambda a, b: jnp.dot(a, b, preferred_element_type=F32)
        gate = dotf(km1, qt) + dotf(km2, qt) + dotf(km3, qt)
        rank = jnp.zeros((n_blk, seq), F32)
        for m in range(n_blk):
            gm_ = gate[m:m + 1, :]
            beats = (m < blk_of_query) & ((gm_ > gate) | ((gm_ == gate) & (m < blk_of_key)))
            rank = rank + jnp.where(beats, 1.0, 0.0)
        chosen = (blk_of_key < blk_of_query) & (rank < MOBA_TOPK)
        bias = jnp.where(chosen, 0.0, NEG_INF)

        for i in range(n_blk):
            n_keys = (i + 1) * MOBA_BLOCK
            s = dotf(k_ref[0:n_keys, :], qt[:, blk(i)])
            tiles = [s[blk(j)] + bias[j:j + 1, blk(i)] for j in range(i)]
            tiles.append(jnp.where(causal, s[blk(i)], NEG_INF))
            top = tiles[0]
            for t in tiles[1:]:
                top = jnp.maximum(top, t)
            m_q = jnp.max(top, axis=0, keepdims=True)
            probs = [jnp.exp2(t - m_q) for t in tiles]
            total = probs[0]
            for pj in probs[1:]:
                total = total + pj
            inv = 1.0 / jnp.sum(total, axis=0, keepdims=True)
            pu = jnp.concatenate([pj.astype(BF16) for pj in probs], axis=0)
            o_ref[i, hs, :] = dotf(vt_ref[hs, 0:n_keys], pu) * inv


def _mix_kernel(x_ref, at_ref, gm_ref, aon_ref, woa_ref, wog_ref, ffn_ref, rwt_ref, rb_ref,
                x1_ref, xs_ref, lpos_ref, gate_ref, tab_ref, tot_ref, carry_ref):
    @pl.when(pl.program_id(0) == 0)
    def _():
        carry_ref[...] = jnp.zeros_like(carry_ref)

    parts = []
    for c in range(TM // MOBA_BLOCK):
        at = at_ref[c]
        ss = jnp.sum(at * at, axis=0, keepdims=True)
        atn = at * lax.rsqrt(ss * (1.0 / ATTN_WIDTH) + NORM_EPS) * aon_ref[...]
        parts.append(atn.T)
    attn = jnp.concatenate(parts, axis=0).astype(BF16)
    x1 = (x_ref[...] + jnp.dot(attn, woa_ref[...], preferred_element_type=F32)
          + jnp.dot(gm_ref[...], wog_ref[...], preferred_element_type=F32))
    x1_ref[...] = x1
    hn = _rms(x1, ffn_ref[...])

    logits = lax.dot_general(rwt_ref[...], hn, (((1,), (1,)), ((), ())),
                             precision=lax.Precision.HIGHEST, preferred_element_type=F32) + rb_ref[...]
    eidx = lax.broadcasted_iota(I32, (N_EXPERTS, TM), 0)
    vals, idxs = [], []
    rest = logits
    for _ in range(TOP_K):
        m = jnp.max(rest, axis=0, keepdims=True)
        idx = jnp.min(jnp.where(rest == m, eidx, N_EXPERTS), axis=0, keepdims=True)
        vals.append(m)
        idxs.append(idx)
        rest = jnp.where(eidx == idx, -jnp.inf, rest)
    exps = [jnp.exp(v - vals[0]) for v in vals]
    denom = exps[0] + exps[1] + exps[2] + exps[3]
    hot = [eidx == idx for idx in idxs]
    multi = jnp.where(hot[0] | hot[1] | hot[2] | hot[3], 1.0, 0.0)

    tp = lax.broadcasted_iota(I32, (TM, TM), 0)
    tq = lax.broadcasted_iota(I32, (TM, TM), 1)
    earlier = jnp.where(tp < tq, 1.0, 0.0).astype(BF16)
    before = jnp.dot(multi.astype(BF16), earlier, preferred_element_type=F32)
    count = jnp.sum(multi, axis=1, keepdims=True)
    seg = jnp.floor((count + (CHUNK - 1)) * (1.0 / CHUNK))
    ep = lax.broadcasted_iota(I32, (N_EXPERTS, N_EXPERTS), 0)
    eq = lax.broadcasted_iota(I32, (N_EXPERTS, N_EXPERTS), 1)
    lower = jnp.where(eq < ep, 1.0, 0.0).astype(BF16)
    seg_wide = jnp.broadcast_to(seg, (N_EXPERTS, LANES))
    off = jnp.dot(lower, seg_wide.astype(BF16), preferred_element_type=F32)
    slot = off[:, 0:1] * CHUNK + before
    lpos = [jnp.sum(jnp.where(hot[kk], slot, 0.0), axis=0, keepdims=True).astype(I32) for kk in range(TOP_K)]
    for kk in range(TOP_K):
        lpos_ref[kk:kk + 1, :] = lpos[kk]
        gate_ref[kk:kk + 1, :] = exps[kk] / denom
    hn_b = hn.astype(BF16)
    for c in range(SORT_ROWS // TM):
        r = lax.broadcasted_iota(I32, (TM, TM), 0) + c * TM
        hit = (r == lpos[0]) | (r == lpos[1]) | (r == lpos[2]) | (r == lpos[3])
        onehot = jnp.where(hit, 1.0, 0.0).astype(BF16)
        xs_ref[c * TM:(c + 1) * TM, :] = jnp.dot(onehot, hn_b, preferred_element_type=F32).astype(BF16)
    lane = lax.broadcasted_iota(I32, (N_EXPERTS, LANES), 1)
    tab = jnp.where(lane == 0, seg_wide, jnp.where(lane == 1, off, carry_ref[...]))
    tab_ref[0] = tab.astype(I32)
    carry_ref[...] = carry_ref[...] + seg
    tot_ref[...] = carry_ref[...].astype(I32)


def _chunk_rows(ref, chunk_index):
    return ref.at[pl.ds(pl.multiple_of(chunk_index * CHUNK, CHUNK), CHUNK), :]


def _expert_kernel(blke_ref, nused_ref, tab_ref, tab_next_ref, xs_ref, wgu_ref, bgu_ref, wdn_ref, bdn_ref, y_ref,
                   xbuf, ybuf, zbuf, wgu_bf, wdn_bf, sem_in, sem_out, sem_fill):
    i = pl.program_id(0)
    n_used = nused_ref[0]
    used = i < n_used
    slot = i % 2

    def copies_in(tab, s):
        return [pltpu.make_async_copy(_chunk_rows(xs_ref, tab[0, 0, c]), _chunk_rows(xbuf.at[s], c), sem_in.at[s])
                for c in range(CHUNKS_PER_BLOCK)]

    def copies_out(s):
        return [pltpu.make_async_copy(_chunk_rows(ybuf.at[s], c), _chunk_rows(y_ref, tab_ref[0, 0, CHUNKS_PER_BLOCK + c]),
                                      sem_out.at[s]) for c in range(CHUNKS_PER_BLOCK)]

    @pl.when(i == 0)
    def _():
        zbuf[...] = jnp.zeros_like(zbuf)
        n_regions = y_ref.shape[0] // SORT_ROWS
        tail = SORT_ROWS - PAIRS
        starts = [t * SORT_ROWS + PAIRS for t in range(n_regions)]
        starts += [n_regions * SORT_ROWS + d * tail for d in range(DUMP_ROWS // tail)]
        fills = [pltpu.make_async_copy(zbuf, y_ref.at[pl.ds(r0, tail), :], sem_fill) for r0 in starts]
        for f in fills:
            f.start()
        for f in fills:
            f.wait()
        for cp in copies_in(tab_ref, 0):
            cp.start()

    @pl.when(i + 1 < n_used)
    def _():
        for cp in copies_in(tab_next_ref, 1 - slot):
            cp.start()

    @pl.when(used & ((i == 0) | (blke_ref[i] != blke_ref[jnp.maximum(i - 1, 0)])))
    def _():
        for c in range(2 * D_EXPERT // FC):
            wgu_bf[:, c * FC:(c + 1) * FC] = wgu_ref[0, :, c * FC:(c + 1) * FC].astype(BF16)
        for c in range(D_EXPERT // FC):
            wdn_bf[c * FC:(c + 1) * FC, :] = wdn_ref[0, c * FC:(c + 1) * FC, :].astype(BF16)

    @pl.when(used)
    def _():
        for cp in copies_in(tab_ref, slot):
            cp.wait()

        @pl.when(i >= 2)
        def _():
            for cp in copies_out(slot):
                cp.wait()

        x = xbuf[slot]
        y = jnp.zeros((RB, D_MODEL), F32) + bdn_ref[0]
        for c in range(D_EXPERT // FC):
            g = jnp.dot(x, wgu_bf[:, c * FC:(c + 1) * FC], preferred_element_type=F32) \
                + bgu_ref[0, :, c * FC:(c + 1) * FC]
            lin = jnp.dot(x, wgu_bf[:, D_EXPERT + c * FC:D_EXPERT + (c + 1) * FC],
                          preferred_element_type=F32) + bgu_ref[0, :, D_EXPERT + c * FC:D_EXPERT + (c + 1) * FC]
            g = jnp.minimum(g, SWIGLU_LIMIT)
            lin = jnp.clip(lin, -SWIGLU_LIMIT, SWIGLU_LIMIT)
            a = g * (1.0 / (1.0 + jnp.exp(-SWIGLU_ALPHA * g))) * (lin + 1.0)
            y = y + jnp.dot(a.astype(BF16), wdn_bf[c * FC:(c + 1) * FC, :], preferred_element_type=F32)
        ybuf[slot] = y.astype(BF16)
        for cp in copies_out(slot):
            cp.start()

        @pl.when(i == n_used - 1)
        def _():
            for cp in copies_out(slot):
                cp.wait()

            @pl.when(i >= 1)
            def _():
                for cp in copies_out(1 - slot):
                    cp.wait()


def _final_kernel(y_ref, x1_ref, lpos_ref, gate_ref, p_ref, plen_ref, wg_ref, wp_ref, post_ref, fin_ref, o_ref):
    lpos = lpos_ref[...]
    gates = gate_ref[...]
    moe = jnp.zeros((TM, D_MODEL), F32)
    for c in range(SORT_ROWS // TM):
        r = lax.broadcasted_iota(I32, (TM, TM), 1) + c * TM
        w = jnp.zeros((TM, TM), F32)
        for kk in range(TOP_K):
            w = w + jnp.where(r == lpos[:, kk:kk + 1], gates[:, kk:kk + 1], 0.0)
        moe = moe + jnp.dot(w.astype(BF16), y_ref[c * TM:(c + 1) * TM, :], preferred_element_type=F32)
    x2 = x1_ref[...] + moe
    z = jnp.dot(_rms(x2, plen_ref[...]).astype(BF16), wg_ref[...], preferred_element_type=F32)
    ple_gate = 1.0 / (1.0 + jnp.exp(-z))
    emb = jnp.dot(p_ref[...].astype(BF16), wp_ref[...], preferred_element_type=F32)
    x3 = x2 + ple_gate * _rms(emb, post_ref[...])
    o_ref[...] = _rms(x3, fin_ref[...])


def _params(sem):
    return pltpu.CompilerParams(dimension_semantics=sem, vmem_limit_bytes=VMEM_LIMIT)


def _rope_tables(seq):
    half = HEAD_DIM // 2
    inv = ROPE_THETA ** (-jnp.arange(half, dtype=F32) / half)
    ang = jnp.arange(seq, dtype=F32)[:, None] * inv[None, :]
    cos = jnp.cos(ang)
    sin = jnp.sin(ang)
    n_heads = ATTN_WIDTH // HEAD_DIM
    cos_t = jnp.tile(jnp.concatenate([cos, cos], axis=1), (1, n_heads))
    sin_t = jnp.tile(jnp.concatenate([-sin, sin], axis=1), (1, n_heads))
    return cos_t, sin_t


def _layer(x, p, mix_norm, w_in, gmlp_ln_g, gmlp_ln_b, gmlp_ws, gmlp_bs, attn_out_norm, gmlp_out_norm,
           w_out, ffn_norm, router_w, router_b, w_gate_up, b_gate_up, w_down, b_down, ple_norm,
           w_ple_gate, w_ple_proj, ple_post_norm, out_norm):
    B, S, D = x.shape
    T = B * S
    A = ATTN_WIDTH
    n_tiles = T // TM
    n_blk = S // MOBA_BLOCK
    tiles_per_seq = S // TM
    xf = x.reshape(T, D)
    row2 = lambda v: v.reshape(1, -1)
    full = lambda shape: pl.BlockSpec(shape, lambda *_: (0,) * len(shape))

    cos_t, sin_t = _rope_tables(S)
    wk = w_in[:, A:2 * A].astype(BF16)
    wqvt = jnp.concatenate([w_in[:, :A], w_in[:, 2 * A:3 * A]], axis=1).T.astype(BF16)
    wuv = w_in[:, 3 * A:].astype(BF16)

    qt, k, vt, kmean, gm = pl.pallas_call(
        _proj_kernel,
        grid=(n_tiles,),
        in_specs=[pl.BlockSpec((TM, D), lambda i: (i, 0)), full((1, D)), full((D, A)), full((2 * A, D)),
                  full((D, 2 * GMLP_WIDTH)),
                  pl.BlockSpec((TM, A), lambda i: (i % tiles_per_seq, 0)),
                  pl.BlockSpec((TM, A), lambda i: (i % tiles_per_seq, 0)),
                  pl.BlockSpec((A, TM), lambda i: (0, i % tiles_per_seq)),
                  pl.BlockSpec((A, TM), lambda i: (0, i % tiles_per_seq)),
                  full((1, GMLP_WIDTH)), full((1, GMLP_WIDTH)),
                  full((GMLP_GROUPS, GMLP_CHUNK, GMLP_CHUNK)), full((GMLP_GROUPS, GMLP_CHUNK, 1)),
                  full((1, GMLP_WIDTH))],
        out_specs=[pl.BlockSpec((A, TM), lambda i: (0, i)), pl.BlockSpec((TM, A), lambda i: (i, 0)),
                   pl.BlockSpec((A, TM), lambda i: (0, i)),
                   pl.BlockSpec((1, TM // MOBA_BLOCK, A), lambda i: (i, 0, 0)),
                   pl.BlockSpec((TM, GMLP_WIDTH), lambda i: (i, 0))],
        out_shape=[jax.ShapeDtypeStruct((A, T), BF16), jax.ShapeDtypeStruct((T, A), BF16),
                   jax.ShapeDtypeStruct((A, T), BF16),
                   jax.ShapeDtypeStruct((n_tiles, TM // MOBA_BLOCK, A), F32),
                   jax.ShapeDtypeStruct((T, GMLP_WIDTH), BF16)],
        scratch_shapes=[pltpu.VMEM((TM, GMLP_WIDTH), F32)],
        compiler_params=_params(("parallel",)),
        name="proj",
    )(xf, row2(mix_norm), wk, wqvt, wuv, cos_t, sin_t, cos_t.T, sin_t.T, row2(gmlp_ln_g), row2(gmlp_ln_b), gmlp_ws,
      gmlp_bs.reshape(GMLP_GROUPS, GMLP_CHUNK, 1), row2(gmlp_out_norm))

    kmean = kmean.reshape(B, n_blk, A)
    n_pairs = A // LANES
    attn_t = pl.pallas_call(
        _attn_kernel,
        grid=(B, n_pairs),
        in_specs=[pl.BlockSpec((LANES, S), lambda b, h: (h, b)), pl.BlockSpec((S, LANES), lambda b, h: (b, h)),
                  pl.BlockSpec((LANES, S), lambda b, h: (h, b)),
                  pl.BlockSpec((1, n_blk, LANES), lambda b, h: (b, 0, h))],
        out_specs=pl.BlockSpec((n_blk, LANES, MOBA_BLOCK), lambda b, h: (b, h, 0)),
        out_shape=jax.ShapeDtypeStruct((T // MOBA_BLOCK, A, MOBA_BLOCK), F32),
        compiler_params=_params(("parallel", "parallel")),
        name="attn",
    )(qt, k, vt, kmean)

    x1, xs, lpos, gates, tab, tot = pl.pallas_call(
        _mix_kernel,
        grid=(n_tiles,),
        in_specs=[pl.BlockSpec((TM, D), lambda i: (i, 0)),
                  pl.BlockSpec((TM // MOBA_BLOCK, A, MOBA_BLOCK), lambda i: (i, 0, 0)),
                  pl.BlockSpec((TM, GMLP_WIDTH), lambda i: (i, 0)),
                  full((A, 1)), full((A, D)), full((GMLP_WIDTH, D)), full((1, D)),
                  full((N_EXPERTS, D)), full((N_EXPERTS, 1))],
        out_specs=[pl.BlockSpec((TM, D), lambda i: (i, 0)), pl.BlockSpec((SORT_ROWS, D), lambda i: (i, 0)),
                   pl.BlockSpec((TOP_K, TM), lambda i: (0, i)), pl.BlockSpec((TOP_K, TM), lambda i: (0, i)),
                   pl.BlockSpec((1, N_EXPERTS, LANES), lambda i: (i, 0, 0)), full((N_EXPERTS, LANES))],
        out_shape=[jax.ShapeDtypeStruct((T, D), F32), jax.ShapeDtypeStruct((n_tiles * SORT_ROWS, D), BF16),
                   jax.ShapeDtypeStruct((TOP_K, T), I32), jax.ShapeDtypeStruct((TOP_K, T), F32),
                   jax.ShapeDtypeStruct((n_tiles, N_EXPERTS, LANES), I32),
                   jax.ShapeDtypeStruct((N_EXPERTS, LANES), I32)],
        scratch_shapes=[pltpu.VMEM((N_EXPERTS, LANES), F32)],
        compiler_params=_params(("arbitrary",)),
        name="mix",
    )(xf, attn_t, gm, attn_out_norm.reshape(A, 1), w_out[:A].astype(BF16), w_out[A:].astype(BF16),
      row2(ffn_norm), router_w.T, router_b.reshape(N_EXPERTS, 1))

    n_blocks = -(-(T * TOP_K + n_tiles * N_EXPERTS * (CHUNK - 1) + N_EXPERTS * (RB - CHUNK)) // RB)
    seg16, off16, before16 = tab[:, :, 0], tab[:, :, 1], tab[:, :, 2]
    tot16 = tot[:, 0]
    blocks_e = (tot16 + CHUNKS_PER_BLOCK - 1) // CHUNKS_PER_BLOCK
    bend = jnp.cumsum(blocks_e)
    nused = bend[-1:].astype(I32)
    bidx = jnp.arange(n_blocks, dtype=I32)
    blk_e = jnp.minimum(jnp.sum(bend[None, :] <= bidx[:, None], axis=1), N_EXPERTS - 1).astype(I32)
    sel = blk_e[:, None] == jnp.arange(N_EXPERTS, dtype=I32)[None, :]
    pick = lambda v: jnp.sum(jnp.where(sel, v[None, :], 0), axis=1)
    chunk_pos = ((bidx - pick(bend - blocks_e)) * CHUNKS_PER_BLOCK)[:, None] \
        + jnp.arange(CHUNKS_PER_BLOCK, dtype=I32)[None, :]
    valid = (chunk_pos < pick(tot16)[:, None]) & (bidx < nused)[:, None]
    seg_end = jnp.sum(jnp.where(sel[:, None, :], (before16 + seg16)[None], 0), axis=2)
    shift = jnp.sum(jnp.where(sel[:, None, :], (off16 - before16)[None], 0), axis=2)
    tile = jnp.minimum(jnp.sum(seg_end[:, None, :] <= chunk_pos[:, :, None], axis=2), n_tiles - 1)
    in_tile = tile[:, :, None] == jnp.arange(n_tiles, dtype=I32)[None, None, :]
    where16 = tile * CHUNKS_PER_TILE + jnp.sum(jnp.where(in_tile, shift[:, None, :], 0), axis=2) + chunk_pos
    dump16 = (n_tiles * CHUNKS_PER_TILE + (bidx % 2)[:, None] * CHUNKS_PER_BLOCK
              + jnp.arange(CHUNKS_PER_BLOCK, dtype=I32)[None, :])
    tabs = jnp.concatenate([jnp.where(valid, where16, 0), jnp.where(valid, where16, dump16)],
                           axis=1).astype(I32).reshape(n_blocks, 1, 2 * CHUNKS_PER_BLOCK)

    last = lambda i, nu: jnp.minimum(i, nu[0] - 1)
    smem_tab = lambda step: pl.BlockSpec((1, 1, 2 * CHUNKS_PER_BLOCK),
                                         lambda i, be, nu: (jnp.minimum(i + step, n_blocks - 1), 0, 0),
                                         memory_space=pltpu.SMEM)
    y = pl.pallas_call(
        _expert_kernel,
        grid_spec=pltpu.PrefetchScalarGridSpec(
            num_scalar_prefetch=2, grid=(n_blocks,),
            in_specs=[smem_tab(0), smem_tab(1), pl.BlockSpec(memory_space=pl.ANY),
                      pl.BlockSpec((1, D, 2 * D_EXPERT), lambda i, be, nu: (be[last(i, nu)], 0, 0)),
                      pl.BlockSpec((1, 1, 2 * D_EXPERT), lambda i, be, nu: (be[last(i, nu)], 0, 0)),
                      pl.BlockSpec((1, D_EXPERT, D), lambda i, be, nu: (be[last(i, nu)], 0, 0)),
                      pl.BlockSpec((1, 1, D), lambda i, be, nu: (be[last(i, nu)], 0, 0))],
            out_specs=pl.BlockSpec(memory_space=pl.ANY),
            scratch_shapes=[pltpu.VMEM((2, RB, D), BF16), pltpu.VMEM((2, RB, D), BF16),
                            pltpu.VMEM((SORT_ROWS - PAIRS, D), BF16),
                            pltpu.VMEM((D, 2 * D_EXPERT), BF16), pltpu.VMEM((D_EXPERT, D), BF16),
                            pltpu.SemaphoreType.DMA((2,)), pltpu.SemaphoreType.DMA((2,)),
                            pltpu.SemaphoreType.DMA(())]),
        out_shape=jax.ShapeDtypeStruct((n_tiles * SORT_ROWS + DUMP_ROWS, D), BF16),
        compiler_params=pltpu.CompilerParams(dimension_semantics=("arbitrary",),
                                             vmem_limit_bytes=EXPERT_VMEM_LIMIT),
        name="experts",
    )(blk_e, nused, tabs, tabs, xs, w_gate_up, b_gate_up.reshape(N_EXPERTS, 1, -1), w_down,
      b_down.reshape(N_EXPERTS, 1, -1))

    out = pl.pallas_call(
        _final_kernel,
        grid=(n_tiles,),
        in_specs=[pl.BlockSpec((SORT_ROWS, D), lambda i: (i, 0)),
                  pl.BlockSpec((TM, D), lambda i: (i, 0)), pl.BlockSpec((TM, TOP_K), lambda i: (i, 0)),
                  pl.BlockSpec((TM, TOP_K), lambda i: (i, 0)),
                  pl.BlockSpec((TM, PLE_DIM), lambda i: (i, 0)), full((1, D)), full((D, D)), full((PLE_DIM, D)),
                  full((1, D)), full((1, D))],
        out_specs=pl.BlockSpec((TM, D), lambda i: (i, 0)),
        out_shape=jax.ShapeDtypeStruct((T, D), F32),
        compiler_params=_params(("parallel",)),
        name="final",
    )(y, x1, lpos.T, gates.T, p.reshape(T, PLE_DIM), row2(ple_norm), w_ple_gate.astype(BF16),
      w_ple_proj.astype(BF16), row2(ple_post_norm), row2(out_norm))
    return out.reshape(B, S, D)


def kernel(x, p, mix_norm, w_in, gmlp_ln_g, gmlp_ln_b, gmlp_ws, gmlp_bs, attn_out_norm, gmlp_out_norm, w_out, ffn_norm, router_w, router_b, w_gate_up, b_gate_up, w_down, b_down, ple_norm, w_ple_gate, w_ple_proj, ple_post_norm, final_norm):
    depth = p.shape[0]
    assert depth == 1, "the final rmsnorm is fused into the (single) layer"
    return _layer(x, p[0], mix_norm[0], w_in[0], gmlp_ln_g[0], gmlp_ln_b[0], gmlp_ws[0], gmlp_bs[0],
                  attn_out_norm[0], gmlp_out_norm[0], w_out[0], ffn_norm[0], router_w[0], router_b[0],
                  w_gate_up[0], b_gate_up[0], w_down[0], b_down[0], ple_norm[0], w_ple_gate[0],
                  w_ple_proj[0], ple_post_norm[0], final_norm)
```

```python
import jax
import jax.numpy as jnp
import numpy as np
from jax import lax
from jax.experimental import pallas as pl
from jax.experimental.pallas import tpu as pltpu

F32 = jnp.float32
BF16 = jnp.bfloat16
I32 = jnp.int32

D_MODEL = 1024
PLE_DIM = 256
ATTN_WIDTH = 512
HEAD_DIM = 64
ROPE_THETA = 10000.0
MOBA_BLOCK = 256
MOBA_TOPK = 3
GMLP_WIDTH = 512
GMLP_GROUPS = 4
GMLP_GROUP_DIM = 128
GMLP_CHUNK = 128
N_EXPERTS = 32
TOP_K = 4
D_EXPERT = 1024
SWIGLU_LIMIT = 7.0
SWIGLU_ALPHA = 1.702
NORM_EPS = 1e-6
NEG_INF = -1e30
Q_SCALE = float(np.log2(np.e) / np.sqrt(HEAD_DIM))

LANES = 128
CHUNK = 16

TM = 512
RB = 512
FC = 512
PAIRS = TM * TOP_K
SORT_ROWS = 2560
CHUNKS_PER_BLOCK = RB // CHUNK
CHUNKS_PER_TILE = SORT_ROWS // CHUNK
DUMP_ROWS = 2 * RB
VMEM_LIMIT = 48 * 1024 * 1024
EXPERT_VMEM_LIMIT = 56 * 1024 * 1024

assert SORT_ROWS % TM == 0 and SORT_ROWS >= PAIRS + N_EXPERTS * (CHUNK - 1)


def _rms(x, g):
    return x * lax.rsqrt(jnp.mean(x * x, axis=-1, keepdims=True) + NORM_EPS) * g


def _nt(a, b):
    return lax.dot_general(a, b, (((1,), (1,)), ((), ())), preferred_element_type=F32)


def _proj_kernel(x_ref, mixn_ref, wk_ref, wqvt_ref, wuv_ref, cos_ref, sin_ref, cost_ref, sint_ref, lng_ref,
                 lnb_ref, ws_ref, bs_ref, gon_ref, qt_ref, k_ref, vt_ref, kmean_ref, gm_ref, gacc_ref):
    h = _rms(x_ref[...], mixn_ref[...]).astype(BF16)
    half = HEAD_DIM // 2

    kx = jnp.dot(h, wk_ref[...], preferred_element_type=F32)
    lane = lax.broadcasted_iota(I32, (TM, LANES), 1)
    first_half = (lane % HEAD_DIM) < half
    parts = []
    for s in range(ATTN_WIDTH // LANES):
        slab = kx[:, s * LANES:(s + 1) * LANES]
        ahead = pltpu.roll(slab, LANES - half, 1)
        behind = pltpu.roll(slab, half, 1)
        parts.append(jnp.where(first_half, ahead, behind))
    k = kx * cos_ref[...] + jnp.concatenate(parts, axis=1) * sin_ref[...]
    k_ref[...] = k.astype(BF16)
    for c in range(TM // MOBA_BLOCK):
        kmean_ref[0, c:c + 1, :] = jnp.mean(k[c * MOBA_BLOCK:(c + 1) * MOBA_BLOCK], axis=0, keepdims=True)

    qvt = _nt(wqvt_ref[...], h)
    qx = qvt[:ATTN_WIDTH]
    swapped = []
    for hd in range(ATTN_WIDTH // HEAD_DIM):
        r0 = hd * HEAD_DIM
        swapped += [qx[r0 + half:r0 + HEAD_DIM], qx[r0:r0 + half]]
    qt = (qx * cost_ref[...] + jnp.concatenate(swapped, axis=0) * sint_ref[...]) * Q_SCALE
    qt_ref[...] = qt.astype(BF16)
    vt_ref[...] = qvt[ATTN_WIDTH:].astype(BF16)

    uv = jnp.dot(h, wuv_ref[...], preferred_element_type=F32)
    inv_sqrt2 = np.float32(1.0 / np.sqrt(2.0))

    def gelu(t):
        return 0.5 * t * (1.0 + lax.erf(t * inv_sqrt2))

    gu = gelu(uv[:, :GMLP_WIDTH])
    gv = gelu(uv[:, GMLP_WIDTH:])
    row = lax.broadcasted_iota(I32, (GMLP_CHUNK, GMLP_CHUNK), 0)
    col = lax.broadcasted_iota(I32, (GMLP_CHUNK, GMLP_CHUNK), 1)
    tril = col <= row
    for g in range(GMLP_GROUPS):
        sl = slice(g * GMLP_GROUP_DIM, (g + 1) * GMLP_GROUP_DIM)
        vg = gv[:, sl]
        mu = jnp.mean(vg, axis=-1, keepdims=True)
        dv = vg - mu
        var = jnp.mean(dv * dv, axis=-1, keepdims=True)
        vn = (dv * lax.rsqrt(var + NORM_EPS) * lng_ref[:, sl] + lnb_ref[:, sl]).astype(BF16)
        w = jnp.where(tril, ws_ref[g], 0.0).astype(BF16)
        bias = bs_ref[g]
        for c in range(TM // GMLP_CHUNK):
            rs = slice(c * GMLP_CHUNK, (c + 1) * GMLP_CHUNK)
            mixed = jnp.dot(w, vn[rs], preferred_element_type=F32) + bias
            gacc_ref[rs, sl] = gu[rs, sl] * mixed
    gm_ref[...] = _rms(gacc_ref[...], gon_ref[...]).astype(BF16)


def _attn_kernel(qt_ref, k_ref, vt_ref, kmean_ref, o_ref):
    seq = k_ref.shape[0]
    n_blk = seq // MOBA_BLOCK
    feat = lax.broadcasted_iota(I32, (LANES, 1), 0)
    lane = lax.broadcasted_iota(I32, (1, LANES), 1)
    blk_of_key = lax.broadcasted_iota(I32, (n_blk, seq), 0)
    blk_of_query = lax.broadcasted_iota(I32, (n_blk, seq), 1) // MOBA_BLOCK
    key_pos = lax.broadcasted_iota(I32, (MOBA_BLOCK, MOBA_BLOCK), 0)
    query_pos = lax.broadcasted_iota(I32, (MOBA_BLOCK, MOBA_BLOCK), 1)
    causal = key_pos <= query_pos
    blk = lambda i: slice(i * MOBA_BLOCK, (i + 1) * MOBA_BLOCK)
    ones = jnp.ones((16, seq), BF16)

    n_heads = LANES // HEAD_DIM
    dotf = lambda a, b: jnp.dot(a, b, preferred_element_type=F32)
    qts, biases = [], []
    for h in range(n_heads):
        rows_in_head = (feat >= h * HEAD_DIM) & (feat < (h + 1) * HEAD_DIM)
        qt = jnp.where(rows_in_head, qt_ref[...], jnp.zeros((), BF16))

        km = jnp.where((lane >= h * HEAD_DIM) & (lane < (h + 1) * HEAD_DIM), kmean_ref[0], 0.0)
        km1 = km.astype(BF16)
        r1 = km - km1.astype(F32)
        km2 = r1.astype(BF16)
        km3 = (r1 - km2.astype(F32)).astype(BF16)
        gate = dotf(km1, qt) + dotf(km2, qt) + dotf(km3, qt)
        rank = jnp.zeros((n_blk, seq), F32)
        for m in range(n_blk):
            gm_ = gate[m:m + 1, :]
            beats = (m < blk_of_query) & ((gm_ > gate) | ((gm_ == gate) & (m < blk_of_key)))
            rank = rank + jnp.where(beats, 1.0, 0.0)
        chosen = (blk_of_key < blk_of_query) & (rank < MOBA_TOPK)
        qts.append(qt)
        biases.append(jnp.where(chosen, 0.0, NEG_INF))

    for i in range(n_blk):
        n_keys = (i + 1) * MOBA_BLOCK
        s_pair = dotf(k_ref[0:n_keys, :], jnp.concatenate([qt[:, blk(i)] for qt in qts], axis=1))
        for h in range(n_heads):
            hs = slice(h * HEAD_DIM, (h + 1) * HEAD_DIM)
            s = s_pair[:, blk(h)]
            bias = biases[h]
            tiles = [s[blk(j)] + bias[j:j + 1, blk(i)] for j in range(i)]
            tiles.append(jnp.where(causal, s[blk(i)], NEG_INF))
            top = tiles[0]
            for t in tiles[1:]:
                top = jnp.maximum(top, t)
            m_q = jnp.max(top, axis=0, keepdims=True)
            pu = jnp.concatenate([jnp.exp2(t - m_q).astype(BF16) for t in tiles], axis=0)
            ov = dotf(jnp.concatenate([vt_ref[hs, 0:n_keys], ones[:, 0:n_keys]], axis=0), pu)
            o_ref[i, hs, :] = ov[0:HEAD_DIM] * (1.0 / ov[HEAD_DIM:HEAD_DIM + 1])


def _mix_kernel(x_ref, at_ref, gm_ref, aon_ref, woa_ref, wog_ref, ffn_ref, rwt_ref, rb_ref,
                x1_ref, xs_ref, lpos_ref, gate_ref, tab_ref, tot_ref, carry_ref):
    @pl.when(pl.program_id(0) == 0)
    def _():
        carry_ref[...] = jnp.zeros_like(carry_ref)

    parts = []
    for c in range(TM // MOBA_BLOCK):
        at = at_ref[c]
        ss = jnp.sum(at * at, axis=0, keepdims=True)
        atn = at * lax.rsqrt(ss * (1.0 / ATTN_WIDTH) + NORM_EPS) * aon_ref[...]
        parts.append(atn.T)
    attn = jnp.concatenate(parts, axis=0).astype(BF16)
    x1 = (x_ref[...] + jnp.dot(attn, woa_ref[...], preferred_element_type=F32)
          + jnp.dot(gm_ref[...], wog_ref[...], preferred_element_type=F32))
    x1_ref[...] = x1
    hn = _rms(x1, ffn_ref[...])

    logits = lax.dot_general(rwt_ref[...], hn, (((1,), (1,)), ((), ())),
                             precision=lax.Precision.HIGHEST, preferred_element_type=F32) + rb_ref[...]
    eidx = lax.broadcasted_iota(I32, (N_EXPERTS, TM), 0)
    vals, idxs = [], []
    rest = logits
    for _ in range(TOP_K):
        m = jnp.max(rest, axis=0, keepdims=True)
        idx = jnp.min(jnp.where(rest == m, eidx, N_EXPERTS), axis=0, keepdims=True)
        vals.append(m)
        idxs.append(idx)
        rest = jnp.where(eidx == idx, -jnp.inf, rest)
    exps = [jnp.exp(v - vals[0]) for v in vals]
    denom = exps[0] + exps[1] + exps[2] + exps[3]
    hot = [eidx == idx for idx in idxs]
    multi = jnp.where(hot[0] | hot[1] | hot[2] | hot[3], 1.0, 0.0)

    tp = lax.broadcasted_iota(I32, (TM, TM), 0)
    tq = lax.broadcasted_iota(I32, (TM, TM), 1)
    earlier = jnp.where(tp < tq, 1.0, 0.0).astype(BF16)
    before = jnp.dot(multi.astype(BF16), earlier, preferred_element_type=F32)
    count = jnp.sum(multi, axis=1, keepdims=True)
    seg = jnp.floor((count + (CHUNK - 1)) * (1.0 / CHUNK))
    ep = lax.broadcasted_iota(I32, (N_EXPERTS, N_EXPERTS), 0)
    eq = lax.broadcasted_iota(I32, (N_EXPERTS, N_EXPERTS), 1)
    lower = jnp.where(eq < ep, 1.0, 0.0).astype(BF16)
    seg_wide = jnp.broadcast_to(seg, (N_EXPERTS, LANES))
    off = jnp.dot(lower, seg_wide.astype(BF16), preferred_element_type=F32)
    slot = off[:, 0:1] * CHUNK + before
    lpos = [jnp.sum(jnp.where(hot[kk], slot, 0.0), axis=0, keepdims=True).astype(I32) for kk in range(TOP_K)]
    for kk in range(TOP_K):
        lpos_ref[kk:kk + 1, :] = lpos[kk]
        gate_ref[kk:kk + 1, :] = exps[kk] / denom
    hn_b = hn.astype(BF16)
    for c in range(SORT_ROWS // TM):
        r = lax.broadcasted_iota(I32, (TM, TM), 0) + c * TM
        onehot = jnp.zeros((TM, TM), F32)
        for kk in range(TOP_K):
            onehot = jnp.where(r == lpos[kk], 1.0, onehot)
        onehot = onehot.astype(BF16)
        xs_ref[c * TM:(c + 1) * TM, :] = jnp.dot(onehot, hn_b, preferred_element_type=F32).astype(BF16)
    lane = lax.broadcasted_iota(I32, (N_EXPERTS, LANES), 1)
    tab = jnp.where(lane == 0, seg_wide, jnp.where(lane == 1, off, carry_ref[...]))
    tab_ref[0] = tab.astype(I32)
    carry_ref[...] = carry_ref[...] + seg
    tot_ref[...] = carry_ref[...].astype(I32)


def _chunk_rows(ref, chunk_index):
    return ref.at[pl.ds(pl.multiple_of(chunk_index * CHUNK, CHUNK), CHUNK), :]


def _expert_kernel(blke_ref, nused_ref, tab_ref, tab_next_ref, xs_ref, wgu_ref, bgu_ref, wdn_ref, bdn_ref, y_ref,
                   xbuf, ybuf, zbuf, wgu_bf, wdn_bf, sem_in, sem_out, sem_fill):
    i = pl.program_id(0)
    n_used = nused_ref[0]
    used = i < n_used
    slot = i % 2

    def copies_in(tab, s):
        return [pltpu.make_async_copy(_chunk_rows(xs_ref, tab[0, 0, c]), _chunk_rows(xbuf.at[s], c), sem_in.at[s])
                for c in range(CHUNKS_PER_BLOCK)]

    def copies_out(s):
        return [pltpu.make_async_copy(_chunk_rows(ybuf.at[s], c), _chunk_rows(y_ref, tab_ref[0, 0, CHUNKS_PER_BLOCK + c]),
                                      sem_out.at[s]) for c in range(CHUNKS_PER_BLOCK)]

    @pl.when(i == 0)
    def _():
        zbuf[...] = jnp.zeros_like(zbuf)
        n_regions = y_ref.shape[0] // SORT_ROWS
        tail = SORT_ROWS - PAIRS
        starts = [t * SORT_ROWS + PAIRS for t in range(n_regions)]
        starts += [n_regions * SORT_ROWS + d * tail for d in range(DUMP_ROWS // tail)]
        fills = [pltpu.make_async_copy(zbuf, y_ref.at[pl.ds(r0, tail), :], sem_fill) for r0 in starts]
        for f in fills:
            f.start()
        for f in fills:
            f.wait()
        for cp in copies_in(tab_ref, 0):
            cp.start()

    @pl.when(used & ((i == 0) | (blke_ref[i] != blke_ref[jnp.maximum(i - 1, 0)])))
    def _():
        for c in range(2 * D_EXPERT // FC):
            wgu_bf[:, c * FC:(c + 1) * FC] = wgu_ref[0, :, c * FC:(c + 1) * FC].astype(BF16)
        for c in range(D_EXPERT // FC):
            wdn_bf[c * FC:(c + 1) * FC, :] = wdn_ref[0, c * FC:(c + 1) * FC, :].astype(BF16)

    @pl.when(used)
    def _():
        for cp in copies_in(tab_ref, slot):
            cp.wait()

        @pl.when(i >= 2)
        def _():
            for cp in copies_out(slot):
                cp.wait()

        for cp in copies_in(tab_next_ref, 1 - slot):
            cp.start()
        x = xbuf[slot]
        y = jnp.zeros((RB, D_MODEL), F32) + bdn_ref[0]
        for c in range(D_EXPERT // FC):
            g = jnp.dot(x, wgu_bf[:, c * FC:(c + 1) * FC], preferred_element_type=F32) \
                + bgu_ref[0, :, c * FC:(c + 1) * FC]
            lin = jnp.dot(x, wgu_bf[:, D_EXPERT + c * FC:D_EXPERT + (c + 1) * FC],
                          preferred_element_type=F32) + bgu_ref[0, :, D_EXPERT + c * FC:D_EXPERT + (c + 1) * FC]
            g = jnp.minimum(g, SWIGLU_LIMIT)
            lin = jnp.clip(lin, -SWIGLU_LIMIT, SWIGLU_LIMIT)
            a = g * (1.0 / (1.0 + jnp.exp(-SWIGLU_ALPHA * g))) * (lin + 1.0)
            y = y + jnp.dot(a.astype(BF16), wdn_bf[c * FC:(c + 1) * FC, :], preferred_element_type=F32)
        ybuf[slot] = y.astype(BF16)
        for cp in copies_out(slot):
            cp.start()

        @pl.when(i == n_used - 1)
        def _():
            for cp in copies_in(tab_next_ref, 1 - slot):
                cp.wait()
            for cp in copies_out(slot):
                cp.wait()

            @pl.when(i >= 1)
            def _():
                for cp in copies_out(1 - slot):
                    cp.wait()


def _final_kernel(y_ref, x1_ref, lpos_ref, gate_ref, p_ref, plen_ref, wg_ref, wp_ref, post_ref, fin_ref, o_ref):
    lpos = lpos_ref[...]
    gates = gate_ref[...]
    moe = jnp.zeros((TM, D_MODEL), F32)
    for c in range(SORT_ROWS // TM):
        r = lax.broadcasted_iota(I32, (TM, TM), 1) + c * TM
        w = jnp.zeros((TM, TM), F32)
        for kk in range(TOP_K):
            w = jnp.where(r == lpos[:, kk:kk + 1], gates[:, kk:kk + 1], w)
        moe = moe + jnp.dot(w.astype(BF16), y_ref[c * TM:(c + 1) * TM, :], preferred_element_type=F32)
    x2 = x1_ref[...] + moe
    z = jnp.dot(_rms(x2, plen_ref[...]).astype(BF16), wg_ref[...], preferred_element_type=F32)
    ple_gate = 1.0 / (1.0 + jnp.exp(-z))
    emb = jnp.dot(p_ref[...].astype(BF16), wp_ref[...], preferred_element_type=F32)
    x3 = x2 + ple_gate * _rms(emb, post_ref[...])
    o_ref[...] = _rms(x3, fin_ref[...])


def _params(sem):
    return pltpu.CompilerParams(dimension_semantics=sem, vmem_limit_bytes=VMEM_LIMIT)


def _rope_tables(seq):
    half = HEAD_DIM // 2
    inv = ROPE_THETA ** (-jnp.arange(half, dtype=F32) / half)
    ang = jnp.arange(seq, dtype=F32)[:, None] * inv[None, :]
    cos = jnp.cos(ang)
    sin = jnp.sin(ang)
    n_heads = ATTN_WIDTH // HEAD_DIM
    cos_t = jnp.tile(jnp.concatenate([cos, cos], axis=1), (1, n_heads))
    sin_t = jnp.tile(jnp.concatenate([-sin, sin], axis=1), (1, n_heads))
    return cos_t, sin_t


def _layer(x, p, mix_norm, w_in, gmlp_ln_g, gmlp_ln_b, gmlp_ws, gmlp_bs, attn_out_norm, gmlp_out_norm,
           w_out, ffn_norm, router_w, router_b, w_gate_up, b_gate_up, w_down, b_down, ple_norm,
           w_ple_gate, w_ple_proj, ple_post_norm, out_norm):
    B, S, D = x.shape
    T = B * S
    A = ATTN_WIDTH
    n_tiles = T // TM
    n_blk = S // MOBA_BLOCK
    tiles_per_seq = S // TM
    xf = x.reshape(T, D)
    row2 = lambda v: v.reshape(1, -1)
    full = lambda shape: pl.BlockSpec(shape, lambda *_: (0,) * len(shape))

    cos_t, sin_t = _rope_tables(S)
    wk = w_in[:, A:2 * A].astype(BF16)
    wqvt = jnp.concatenate([w_in[:, :A], w_in[:, 2 * A:3 * A]], axis=1).T.astype(BF16)
    wuv = w_in[:, 3 * A:].astype(BF16)

    qt, k, vt, kmean, gm = pl.pallas_call(
        _proj_kernel,
        grid=(n_tiles,),
        in_specs=[pl.BlockSpec((TM, D), lambda i: (i, 0)), full((1, D)), full((D, A)), full((2 * A, D)),
                  full((D, 2 * GMLP_WIDTH)),
                  pl.BlockSpec((TM, A), lambda i: (i % tiles_per_seq, 0)),
                  pl.BlockSpec((TM, A), lambda i: (i % tiles_per_seq, 0)),
                  pl.BlockSpec((A, TM), lambda i: (0, i % tiles_per_seq)),
                  pl.BlockSpec((A, TM), lambda i: (0, i % tiles_per_seq)),
                  full((1, GMLP_WIDTH)), full((1, GMLP_WIDTH)),
                  full((GMLP_GROUPS, GMLP_CHUNK, GMLP_CHUNK)), full((GMLP_GROUPS, GMLP_CHUNK, 1)),
                  full((1, GMLP_WIDTH))],
        out_specs=[pl.BlockSpec((A, TM), lambda i: (0, i)), pl.BlockSpec((TM, A), lambda i: (i, 0)),
                   pl.BlockSpec((A, TM), lambda i: (0, i)),
                   pl.BlockSpec((1, TM // MOBA_BLOCK, A), lambda i: (i, 0, 0)),
                   pl.BlockSpec((TM, GMLP_WIDTH), lambda i: (i, 0))],
        out_shape=[jax.ShapeDtypeStruct((A, T), BF16), jax.ShapeDtypeStruct((T, A), BF16),
                   jax.ShapeDtypeStruct((A, T), BF16),
                   jax.ShapeDtypeStruct((n_tiles, TM // MOBA_BLOCK, A), F32),
                   jax.ShapeDtypeStruct((T, GMLP_WIDTH), BF16)],
        scratch_shapes=[pltpu.VMEM((TM, GMLP_WIDTH), F32)],
        compiler_params=_params(("parallel",)),
        name="proj",
    )(xf, row2(mix_norm), wk, wqvt, wuv, cos_t, sin_t, cos_t.T, sin_t.T, row2(gmlp_ln_g), row2(gmlp_ln_b), gmlp_ws,
      gmlp_bs.reshape(GMLP_GROUPS, GMLP_CHUNK, 1), row2(gmlp_out_norm))

    kmean = kmean.reshape(B, n_blk, A)
    n_pairs = A // LANES
    attn_t = pl.pallas_call(
        _attn_kernel,
        grid=(B, n_pairs),
        in_specs=[pl.BlockSpec((LANES, S), lambda b, h: (h, b)), pl.BlockSpec((S, LANES), lambda b, h: (b, h)),
                  pl.BlockSpec((LANES, S), lambda b, h: (h, b)),
                  pl.BlockSpec((1, n_blk, LANES), lambda b, h: (b, 0, h))],
        out_specs=pl.BlockSpec((n_blk, LANES, MOBA_BLOCK), lambda b, h: (b, h, 0)),
        out_shape=jax.ShapeDtypeStruct((T // MOBA_BLOCK, A, MOBA_BLOCK), F32),
        compiler_params=_params(("parallel", "parallel")),
        name="attn",
    )(qt, k, vt, kmean)

    x1, xs, lpos, gates, tab, tot = pl.pallas_call(
        _mix_kernel,
        grid=(n_tiles,),
        in_specs=[pl.BlockSpec((TM, D), lambda i: (i, 0)),
                  pl.BlockSpec((TM // MOBA_BLOCK, A, MOBA_BLOCK), lambda i: (i, 0, 0)),
                  pl.BlockSpec((TM, GMLP_WIDTH), lambda i: (i, 0)),
                  full((A, 1)), full((A, D)), full((GMLP_WIDTH, D)), full((1, D)),
                  full((N_EXPERTS, D)), full((N_EXPERTS, 1))],
        out_specs=[pl.BlockSpec((TM, D), lambda i: (i, 0)), pl.BlockSpec((SORT_ROWS, D), lambda i: (i, 0)),
                   pl.BlockSpec((TOP_K, TM), lambda i: (0, i)), pl.BlockSpec((TOP_K, TM), lambda i: (0, i)),
                   pl.BlockSpec((1, N_EXPERTS, LANES), lambda i: (i, 0, 0)), full((N_EXPERTS, LANES))],
        out_shape=[jax.ShapeDtypeStruct((T, D), F32), jax.ShapeDtypeStruct((n_tiles * SORT_ROWS, D), BF16),
                   jax.ShapeDtypeStruct((TOP_K, T), I32), jax.ShapeDtypeStruct((TOP_K, T), F32),
                   jax.ShapeDtypeStruct((n_tiles, N_EXPERTS, LANES), I32),
                   jax.ShapeDtypeStruct((N_EXPERTS, LANES), I32)],
        scratch_shapes=[pltpu.VMEM((N_EXPERTS, LANES), F32)],
        compiler_params=_params(("arbitrary",)),
        name="mix",
    )(xf, attn_t, gm, attn_out_norm.reshape(A, 1), w_out[:A].astype(BF16), w_out[A:].astype(BF16),
      row2(ffn_norm), router_w.T, router_b.reshape(N_EXPERTS, 1))

    n_blocks = -(-(T * TOP_K + n_tiles * N_EXPERTS * (CHUNK - 1) + N_EXPERTS * (RB - CHUNK)) // RB)
    seg16, off16, before16 = tab[:, :, 0], tab[:, :, 1], tab[:, :, 2]
    tot16 = tot[:, 0]
    blocks_e = (tot16 + CHUNKS_PER_BLOCK - 1) // CHUNKS_PER_BLOCK
    bend = jnp.cumsum(blocks_e)
    nused = bend[-1:].astype(I32)
    bidx = jnp.arange(n_blocks, dtype=I32)
    blk_e = jnp.minimum(jnp.sum(bend[None, :] <= bidx[:, None], axis=1), N_EXPERTS - 1).astype(I32)
    sel = blk_e[:, None] == jnp.arange(N_EXPERTS, dtype=I32)[None, :]
    pick = lambda v: jnp.sum(jnp.where(sel, v[None, :], 0), axis=1)
    chunk_pos = ((bidx - pick(bend - blocks_e)) * CHUNKS_PER_BLOCK)[:, None] \
        + jnp.arange(CHUNKS_PER_BLOCK, dtype=I32)[None, :]
    valid = (chunk_pos < pick(tot16)[:, None]) & (bidx < nused)[:, None]
    seg_end = jnp.sum(jnp.where(sel[:, None, :], (before16 + seg16)[None], 0), axis=2)
    shift = jnp.sum(jnp.where(sel[:, None, :], (off16 - before16)[None], 0), axis=2)
    tile = jnp.minimum(jnp.sum(seg_end[:, None, :] <= chunk_pos[:, :, None], axis=2), n_tiles - 1)
    in_tile = tile[:, :, None] == jnp.arange(n_tiles, dtype=I32)[None, None, :]
    where16 = tile * CHUNKS_PER_TILE + jnp.sum(jnp.where(in_tile, shift[:, None, :], 0), axis=2) + chunk_pos
    dump16 = (n_tiles * CHUNKS_PER_TILE + (bidx % 2)[:, None] * CHUNKS_PER_BLOCK
              + jnp.arange(CHUNKS_PER_BLOCK, dtype=I32)[None, :])
    tabs = jnp.concatenate([jnp.where(valid, where16, 0), jnp.where(valid, where16, dump16)],
                           axis=1).astype(I32).reshape(n_blocks, 1, 2 * CHUNKS_PER_BLOCK)

    last = lambda i, nu: jnp.minimum(i, nu[0] - 1)
    smem_tab = lambda step: pl.BlockSpec((1, 1, 2 * CHUNKS_PER_BLOCK),
                                         lambda i, be, nu: (jnp.minimum(i + step, n_blocks - 1), 0, 0),
                                         memory_space=pltpu.SMEM)
    y = pl.pallas_call(
        _expert_kernel,
        grid_spec=pltpu.PrefetchScalarGridSpec(
            num_scalar_prefetch=2, grid=(n_blocks,),
            in_specs=[smem_tab(0), smem_tab(1), pl.BlockSpec(memory_space=pl.ANY),
                      pl.BlockSpec((1, D, 2 * D_EXPERT), lambda i, be, nu: (be[last(i, nu)], 0, 0)),
                      pl.BlockSpec((1, 1, 2 * D_EXPERT), lambda i, be, nu: (be[last(i, nu)], 0, 0)),
                      pl.BlockSpec((1, D_EXPERT, D), lambda i, be, nu: (be[last(i, nu)], 0, 0)),
                      pl.BlockSpec((1, 1, D), lambda i, be, nu: (be[last(i, nu)], 0, 0))],
            out_specs=pl.BlockSpec(memory_space=pl.ANY),
            scratch_shapes=[pltpu.VMEM((2, RB, D), BF16), pltpu.VMEM((2, RB, D), BF16),
                            pltpu.VMEM((SORT_ROWS - PAIRS, D), BF16),
                            pltpu.VMEM((D, 2 * D_EXPERT), BF16), pltpu.VMEM((D_EXPERT, D), BF16),
                            pltpu.SemaphoreType.DMA((2,)), pltpu.SemaphoreType.DMA((2,)),
                            pltpu.SemaphoreType.DMA(())]),
        out_shape=jax.ShapeDtypeStruct((n_tiles * SORT_ROWS + DUMP_ROWS, D), BF16),
        compiler_params=pltpu.CompilerParams(dimension_semantics=("arbitrary",),
                                             vmem_limit_bytes=EXPERT_VMEM_LIMIT),
        name="experts",
    )(blk_e, nused, tabs, tabs, xs, w_gate_up, b_gate_up.reshape(N_EXPERTS, 1, -1), w_down,
      b_down.reshape(N_EXPERTS, 1, -1))

    out = pl.pallas_call(
        _final_kernel,
        grid=(n_tiles,),
        in_specs=[pl.BlockSpec((SORT_ROWS, D), lambda i: (i, 0)),
                  pl.BlockSpec((TM, D), lambda i: (i, 0)), pl.BlockSpec((TM, TOP_K), lambda i: (i, 0)),
                  pl.BlockSpec((TM, TOP_K), lambda i: (i, 0)),
                  pl.BlockSpec((TM, PLE_DIM), lambda i: (i, 0)), full((1, D)), full((D, D)), full((PLE_DIM, D)),
                  full((1, D)), full((1, D))],
        out_specs=pl.BlockSpec((TM, D), lambda i: (i, 0)),
        out_shape=jax.ShapeDtypeStruct((T, D), F32),
        compiler_params=_params(("parallel",)),
        name="final",
    )(y, x1, lpos.T, gates.T, p.reshape(T, PLE_DIM), row2(ple_norm), w_ple_gate.astype(BF16),
      w_ple_proj.astype(BF16), row2(ple_post_norm), row2(out_norm))
    return out.reshape(B, S, D)


def kernel(x, p, mix_norm, w_in, gmlp_ln_g, gmlp_ln_b, gmlp_ws, gmlp_bs, attn_out_norm, gmlp_out_norm, w_out, ffn_norm, router_w, router_b, w_gate_up, b_gate_up, w_down, b_down, ple_norm, w_ple_gate, w_ple_proj, ple_post_norm, final_norm):
    depth = p.shape[0]
    assert depth == 1, "the final rmsnorm is fused into the (single) layer"
    return _layer(x, p[0], mix_norm[0], w_in[0], gmlp_ln_g[0], gmlp_ln_b[0], gmlp_ws[0], gmlp_bs[0],
                  attn_out_norm[0], gmlp_out_norm[0], w_out[0], ffn_norm[0], router_w[0], router_b[0],
                  w_gate_up[0], b_gate_up[0], w_down[0], b_down[0], ple_norm[0], w_ple_gate[0],
                  w_ple_proj[0], ple_post_norm[0], final_norm)
```

```python
import jax
import jax.numpy as jnp
import numpy as np
from jax import lax
from jax.experimental import pallas as pl
from jax.experimental.pallas import tpu as pltpu

F32 = jnp.float32
BF16 = jnp.bfloat16
I32 = jnp.int32

D_MODEL = 1024
PLE_DIM = 256
ATTN_WIDTH = 512
HEAD_DIM = 64
ROPE_THETA = 10000.0
MOBA_BLOCK = 256
MOBA_TOPK = 3
GMLP_WIDTH = 512
GMLP_GROUPS = 4
GMLP_GROUP_DIM = 128
GMLP_CHUNK = 128
N_EXPERTS = 32
TOP_K = 4
D_EXPERT = 1024
SWIGLU_LIMIT = 7.0
SWIGLU_ALPHA = 1.702
NORM_EPS = 1e-6
NEG_INF = -1e30
Q_SCALE = float(np.log2(np.e) / np.sqrt(HEAD_DIM))

LANES = 128
CHUNK = 16

TM = 512
RB = 512
FC = 512
PAIRS = TM * TOP_K
SORT_ROWS = 2560
CHUNKS_PER_BLOCK = RB // CHUNK
CHUNKS_PER_TILE = SORT_ROWS // CHUNK
DUMP_ROWS = 2 * RB
VMEM_LIMIT = 48 * 1024 * 1024
EXPERT_VMEM_LIMIT = 56 * 1024 * 1024

assert SORT_ROWS % TM == 0 and SORT_ROWS >= PAIRS + N_EXPERTS * (CHUNK - 1)


def _rms(x, g):
    return x * lax.rsqrt(jnp.mean(x * x, axis=-1, keepdims=True) + NORM_EPS) * g


def _nt(a, b):
    return lax.dot_general(a, b, (((1,), (1,)), ((), ())), preferred_element_type=F32)


def _proj_kernel(x_ref, mixn_ref, wk_ref, wqvt_ref, wuv_ref, cos_ref, sin_ref, cost_ref, sint_ref, lng_ref,
                 lnb_ref, ws_ref, bs_ref, gon_ref, qt_ref, k_ref, vt_ref, kmean_ref, gm_ref, gacc_ref):
    h = _rms(x_ref[...], mixn_ref[...]).astype(BF16)
    half = HEAD_DIM // 2

    kx = jnp.dot(h, wk_ref[...], preferred_element_type=F32)
    lane = lax.broadcasted_iota(I32, (TM, LANES), 1)
    first_half = (lane % HEAD_DIM) < half
    parts = []
    for s in range(ATTN_WIDTH // LANES):
        slab = kx[:, s * LANES:(s + 1) * LANES]
        ahead = pltpu.roll(slab, LANES - half, 1)
        behind = pltpu.roll(slab, half, 1)
        parts.append(jnp.where(first_half, ahead, behind))
    k = kx * cos_ref[...] + jnp.concatenate(parts, axis=1) * sin_ref[...]
    k_ref[...] = k.astype(BF16)
    for c in range(TM // MOBA_BLOCK):
        kmean_ref[0, c:c + 1, :] = jnp.mean(k[c * MOBA_BLOCK:(c + 1) * MOBA_BLOCK], axis=0, keepdims=True)

    qvt = _nt(wqvt_ref[...], h)
    qx = qvt[:ATTN_WIDTH]
    swapped = []
    for hd in range(ATTN_WIDTH // HEAD_DIM):
        r0 = hd * HEAD_DIM
        swapped += [qx[r0 + half:r0 + HEAD_DIM], qx[r0:r0 + half]]
    qt = (qx * cost_ref[...] + jnp.concatenate(swapped, axis=0) * sint_ref[...]) * Q_SCALE
    qt_ref[...] = qt.astype(BF16)
    vt_ref[...] = qvt[ATTN_WIDTH:].astype(BF16)

    uv = jnp.dot(h, wuv_ref[...], preferred_element_type=F32)
    inv_sqrt2 = np.float32(1.0 / np.sqrt(2.0))

    def gelu(t):
        return 0.5 * t * (1.0 + lax.erf(t * inv_sqrt2))

    gu = gelu(uv[:, :GMLP_WIDTH])
    gv = gelu(uv[:, GMLP_WIDTH:])
    row = lax.broadcasted_iota(I32, (GMLP_CHUNK, GMLP_CHUNK), 0)
    col = lax.broadcasted_iota(I32, (GMLP_CHUNK, GMLP_CHUNK), 1)
    tril = col <= row
    for g in range(GMLP_GROUPS):
        sl = slice(g * GMLP_GROUP_DIM, (g + 1) * GMLP_GROUP_DIM)
        vg = gv[:, sl]
        mu = jnp.mean(vg, axis=-1, keepdims=True)
        dv = vg - mu
        var = jnp.mean(dv * dv, axis=-1, keepdims=True)
        vn = (dv * lax.rsqrt(var + NORM_EPS) * lng_ref[:, sl] + lnb_ref[:, sl]).astype(BF16)
        w = jnp.where(tril, ws_ref[g], 0.0).astype(BF16)
        bias = bs_ref[g]
        for c in range(TM // GMLP_CHUNK):
            rs = slice(c * GMLP_CHUNK, (c + 1) * GMLP_CHUNK)
            mixed = jnp.dot(w, vn[rs], preferred_element_type=F32) + bias
            gacc_ref[rs, sl] = gu[rs, sl] * mixed
    gm_ref[...] = _rms(gacc_ref[...], gon_ref[...]).astype(BF16)


def _attn_kernel(qt_ref, k_ref, vt_ref, kmean_ref, o_ref):
    seq = k_ref.shape[0]
    n_blk = seq // MOBA_BLOCK
    feat = lax.broadcasted_iota(I32, (LANES, 1), 0)
    lane = lax.broadcasted_iota(I32, (1, LANES), 1)
    blk_of_key = lax.broadcasted_iota(I32, (n_blk, seq), 0)
    blk_of_query = lax.broadcasted_iota(I32, (n_blk, seq), 1) // MOBA_BLOCK
    key_pos = lax.broadcasted_iota(I32, (MOBA_BLOCK, MOBA_BLOCK), 0)
    query_pos = lax.broadcasted_iota(I32, (MOBA_BLOCK, MOBA_BLOCK), 1)
    causal = key_pos <= query_pos
    blk = lambda i: slice(i * MOBA_BLOCK, (i + 1) * MOBA_BLOCK)
    ones = jnp.ones((16, seq), BF16)

    n_heads = LANES // HEAD_DIM
    dotf = lambda a, b: jnp.dot(a, b, preferred_element_type=F32)
    qts, biases = [], []
    for h in range(n_heads):
        rows_in_head = (feat >= h * HEAD_DIM) & (feat < (h + 1) * HEAD_DIM)
        qt = jnp.where(rows_in_head, qt_ref[...], jnp.zeros((), BF16))

        km = jnp.where((lane >= h * HEAD_DIM) & (lane < (h + 1) * HEAD_DIM), kmean_ref[0], 0.0)
        km1 = km.astype(BF16)
        r1 = km - km1.astype(F32)
        km2 = r1.astype(BF16)
        km3 = (r1 - km2.astype(F32)).astype(BF16)
        gate = dotf(km1, qt) + dotf(km2, qt) + dotf(km3, qt)
        rank = jnp.zeros((n_blk, seq), F32)
        for m in range(n_blk):
            gm_ = gate[m:m + 1, :]
            beats = (m < blk_of_query) & ((gm_ > gate) | ((gm_ == gate) & (m < blk_of_key)))
            rank = rank + jnp.where(beats, 1.0, 0.0)
        chosen = (blk_of_key < blk_of_query) & (rank < MOBA_TOPK)
        qts.append(qt)
        biases.append(jnp.where(chosen, 0.0, NEG_INF))

    def scores(i):
        return dotf(k_ref[0:(i + 1) * MOBA_BLOCK, :], jnp.concatenate([qt[:, blk(i)] for qt in qts], axis=1))

    order = list(range(n_blk - 1, -1, -1))
    s_next = scores(order[0])
    for pos, i in enumerate(order):
        n_keys = (i + 1) * MOBA_BLOCK
        s_pair = s_next
        if pos + 1 < n_blk:
            s_next = scores(order[pos + 1])
        for h in range(n_heads):
            hs = slice(h * HEAD_DIM, (h + 1) * HEAD_DIM)
            s = s_pair[:, blk(h)]
            bias = biases[h]
            tiles = [s[blk(j)] + bias[j:j + 1, blk(i)] for j in range(i)]
            tiles.append(jnp.where(causal, s[blk(i)], NEG_INF))
            top = tiles[0]
            for t in tiles[1:]:
                top = jnp.maximum(top, t)
            m_q = jnp.max(top, axis=0, keepdims=True)
            pu = jnp.concatenate([jnp.exp2(t - m_q).astype(BF16) for t in tiles], axis=0)
            ov = dotf(jnp.concatenate([vt_ref[hs, 0:n_keys], ones[:, 0:n_keys]], axis=0), pu)
            o_ref[i, hs, :] = ov[0:HEAD_DIM] * (1.0 / ov[HEAD_DIM:HEAD_DIM + 1])


def _mix_kernel(x_ref, at_ref, gm_ref, aon_ref, woa_ref, wog_ref, ffn_ref, rwt_ref, rb_ref,
                x1_ref, xs_ref, lpos_ref, gate_ref, tab_ref, tot_ref, carry_ref):
    @pl.when(pl.program_id(0) == 0)
    def _():
        carry_ref[...] = jnp.zeros_like(carry_ref)

    parts = []
    for c in range(TM // MOBA_BLOCK):
        at = at_ref[c]
        ss = jnp.sum(at * at, axis=0, keepdims=True)
        atn = at * lax.rsqrt(ss * (1.0 / ATTN_WIDTH) + NORM_EPS) * aon_ref[...]
        parts.append(atn.T)
    attn = jnp.concatenate(parts, axis=0).astype(BF16)
    x1 = (x_ref[...] + jnp.dot(attn, woa_ref[...], preferred_element_type=F32)
          + jnp.dot(gm_ref[...], wog_ref[...], preferred_element_type=F32))
    x1_ref[...] = x1
    hn = _rms(x1, ffn_ref[...])

    hn_b = hn.astype(BF16)
    hn_lo = (hn - hn_b.astype(F32)).astype(BF16)
    rw = rwt_ref[...]
    rw_hi = rw.astype(BF16)
    rw_lo = (rw - rw_hi.astype(F32)).astype(BF16)
    both = _nt(jnp.concatenate([rw_hi, rw_lo], axis=0), hn_b)
    logits = both[:N_EXPERTS] + both[N_EXPERTS:] + _nt(rw_hi, hn_lo) + rb_ref[...]
    eidx = lax.broadcasted_iota(I32, (N_EXPERTS, TM), 0)
    vals, idxs = [], []
    rest = logits
    for _ in range(TOP_K):
        m = jnp.max(rest, axis=0, keepdims=True)
        idx = jnp.min(jnp.where(rest == m, eidx, N_EXPERTS), axis=0, keepdims=True)
        vals.append(m)
        idxs.append(idx)
        rest = jnp.where(eidx == idx, -jnp.inf, rest)
    exps = [jnp.exp(v - vals[0]) for v in vals]
    denom = exps[0] + exps[1] + exps[2] + exps[3]
    hot = [eidx == idx for idx in idxs]
    multi = jnp.where(hot[0] | hot[1] | hot[2] | hot[3], 1.0, 0.0)

    tp = lax.broadcasted_iota(I32, (TM, TM), 0)
    tq = lax.broadcasted_iota(I32, (TM, TM), 1)
    earlier = jnp.where(tp < tq, 1.0, 0.0).astype(BF16)
    before = jnp.dot(multi.astype(BF16), earlier, preferred_element_type=F32)
    count = jnp.sum(multi, axis=1, keepdims=True)
    seg = jnp.floor((count + (CHUNK - 1)) * (1.0 / CHUNK))
    ep = lax.broadcasted_iota(I32, (N_EXPERTS, N_EXPERTS), 0)
    eq = lax.broadcasted_iota(I32, (N_EXPERTS, N_EXPERTS), 1)
    lower = jnp.where(eq < ep, 1.0, 0.0).astype(BF16)
    seg_wide = jnp.broadcast_to(seg, (N_EXPERTS, LANES))
    off = jnp.dot(lower, seg_wide.astype(BF16), preferred_element_type=F32)
    slot = off[:, 0:1] * CHUNK + before
    lpos = [jnp.sum(jnp.where(hot[kk], slot, 0.0), axis=0, keepdims=True).astype(I32) for kk in range(TOP_K)]
    for kk in range(TOP_K):
        lpos_ref[kk:kk + 1, :] = lpos[kk]
        gate_ref[kk:kk + 1, :] = exps[kk] / denom
    def onehot_rows(c):
        r = lax.broadcasted_iota(I32, (TM, TM), 0) + c * TM
        onehot = jnp.zeros((TM, TM), F32)
        for kk in range(TOP_K):
            onehot = jnp.where(r == lpos[kk], 1.0, onehot)
        return onehot.astype(BF16)

    n_chunks = SORT_ROWS // TM
    nxt = onehot_rows(0)
    for c in range(n_chunks):
        cur = nxt
        if c + 1 < n_chunks:
            nxt = onehot_rows(c + 1)
        xs_ref[c * TM:(c + 1) * TM, :] = jnp.dot(cur, hn_b, preferred_element_type=F32).astype(BF16)
    lane = lax.broadcasted_iota(I32, (N_EXPERTS, LANES), 1)
    tab = jnp.where(lane == 0, seg_wide, jnp.where(lane == 1, off, carry_ref[...]))
    tab_ref[0] = tab.astype(I32)
    carry_ref[...] = carry_ref[...] + seg
    tot_ref[...] = carry_ref[...].astype(I32)


def _chunk_rows(ref, chunk_index):
    return ref.at[pl.ds(pl.multiple_of(chunk_index * CHUNK, CHUNK), CHUNK), :]


def _expert_kernel(blke_ref, nused_ref, tab_ref, tab_next_ref, xs_ref, wgu_ref, bgu_ref, wdn_ref, bdn_ref, y_ref,
                   xbuf, ybuf, zbuf, wgu_bf, wdn_bf, sem_in, sem_out, sem_fill):
    i = pl.program_id(0)
    n_used = nused_ref[0]
    used = i < n_used
    slot = i % 2

    def copies_in(tab, s):
        return [pltpu.make_async_copy(_chunk_rows(xs_ref, tab[0, 0, c]), _chunk_rows(xbuf.at[s], c), sem_in.at[s])
                for c in range(CHUNKS_PER_BLOCK)]

    def copies_out(s):
        return [pltpu.make_async_copy(_chunk_rows(ybuf.at[s], c), _chunk_rows(y_ref, tab_ref[0, 0, CHUNKS_PER_BLOCK + c]),
                                      sem_out.at[s]) for c in range(CHUNKS_PER_BLOCK)]

    @pl.when(i == 0)
    def _():
        zbuf[...] = jnp.zeros_like(zbuf)
        n_regions = y_ref.shape[0] // SORT_ROWS
        tail = SORT_ROWS - PAIRS
        starts = [t * SORT_ROWS + PAIRS for t in range(n_regions)]
        starts += [n_regions * SORT_ROWS + d * tail for d in range(DUMP_ROWS // tail)]
        fills = [pltpu.make_async_copy(zbuf, y_ref.at[pl.ds(r0, tail), :], sem_fill) for r0 in starts]
        for f in fills:
            f.start()
        for f in fills:
            f.wait()
        for cp in copies_in(tab_ref, 0):
            cp.start()

    @pl.when(used & ((i == 0) | (blke_ref[i] != blke_ref[jnp.maximum(i - 1, 0)])))
    def _():
        for c in range(2 * D_EXPERT // FC):
            wgu_bf[:, c * FC:(c + 1) * FC] = wgu_ref[0, :, c * FC:(c + 1) * FC].astype(BF16)
        for c in range(D_EXPERT // FC):
            wdn_bf[c * FC:(c + 1) * FC, :] = wdn_ref[0, c * FC:(c + 1) * FC, :].astype(BF16)

    @pl.when(used)
    def _():
        for cp in copies_in(tab_ref, slot):
            cp.wait()

        @pl.when(i >= 2)
        def _():
            for cp in copies_out(slot):
                cp.wait()

        for cp in copies_in(tab_next_ref, 1 - slot):
            cp.start()
        x = xbuf[slot]
        y = jnp.zeros((RB, D_MODEL), F32) + bdn_ref[0]
        for c in range(D_EXPERT // FC):
            g = jnp.dot(x, wgu_bf[:, c * FC:(c + 1) * FC], preferred_element_type=F32) \
                + bgu_ref[0, :, c * FC:(c + 1) * FC]
            lin = jnp.dot(x, wgu_bf[:, D_EXPERT + c * FC:D_EXPERT + (c + 1) * FC],
                          preferred_element_type=F32) + bgu_ref[0, :, D_EXPERT + c * FC:D_EXPERT + (c + 1) * FC]
            g = jnp.minimum(g, SWIGLU_LIMIT)
            lin = jnp.clip(lin, -SWIGLU_LIMIT, SWIGLU_LIMIT)
            a = g * (1.0 / (1.0 + jnp.exp(-SWIGLU_ALPHA * g))) * (lin + 1.0)
            y = y + jnp.dot(a.astype(BF16), wdn_bf[c * FC:(c + 1) * FC, :], preferred_element_type=F32)
        ybuf[slot] = y.astype(BF16)
        for cp in copies_out(slot):
            cp.start()

        @pl.when(i == n_used - 1)
        def _():
            for cp in copies_in(tab_next_ref, 1 - slot):
                cp.wait()
            for cp in copies_out(slot):
                cp.wait()

            @pl.when(i >= 1)
            def _():
                for cp in copies_out(1 - slot):
                    cp.wait()


def _final_kernel(y_ref, x1_ref, lpos_ref, gate_ref, p_ref, plen_ref, wg_ref, wp_ref, post_ref, fin_ref, o_ref):
    lpos = lpos_ref[...]
    gates = gate_ref[...]
    def gate_cols(c):
        r = lax.broadcasted_iota(I32, (TM, TM), 1) + c * TM
        w = jnp.zeros((TM, TM), F32)
        for kk in range(TOP_K):
            w = jnp.where(r == lpos[:, kk:kk + 1], gates[:, kk:kk + 1], w)
        return w.astype(BF16)

    emb = jnp.dot(p_ref[...].astype(BF16), wp_ref[...], preferred_element_type=F32)
    n_chunks = SORT_ROWS // TM
    moe = jnp.zeros((TM, D_MODEL), F32)
    nxt = gate_cols(0)
    for c in range(n_chunks):
        cur = nxt
        if c + 1 < n_chunks:
            nxt = gate_cols(c + 1)
        moe = moe + jnp.dot(cur, y_ref[c * TM:(c + 1) * TM, :], preferred_element_type=F32)
    x2 = x1_ref[...] + moe
    z = jnp.dot(_rms(x2, plen_ref[...]).astype(BF16), wg_ref[...], preferred_element_type=F32)
    ple_gate = 1.0 / (1.0 + jnp.exp(-z))
    x3 = x2 + ple_gate * _rms(emb, post_ref[...])
    o_ref[...] = _rms(x3, fin_ref[...])


def _params(sem):
    return pltpu.CompilerParams(dimension_semantics=sem, vmem_limit_bytes=VMEM_LIMIT)


def _rope_tables(seq):
    half = HEAD_DIM // 2
    inv = ROPE_THETA ** (-jnp.arange(half, dtype=F32) / half)
    ang = jnp.arange(seq, dtype=F32)[:, None] * inv[None, :]
    cos = jnp.cos(ang)
    sin = jnp.sin(ang)
    n_heads = ATTN_WIDTH // HEAD_DIM
    cos_t = jnp.tile(jnp.concatenate([cos, cos], axis=1), (1, n_heads))
    sin_t = jnp.tile(jnp.concatenate([-sin, sin], axis=1), (1, n_heads))
    return cos_t, sin_t


def _layer(x, p, mix_norm, w_in, gmlp_ln_g, gmlp_ln_b, gmlp_ws, gmlp_bs, attn_out_norm, gmlp_out_norm,
           w_out, ffn_norm, router_w, router_b, w_gate_up, b_gate_up, w_down, b_down, ple_norm,
           w_ple_gate, w_ple_proj, ple_post_norm, out_norm):
    B, S, D = x.shape
    T = B * S
    A = ATTN_WIDTH
    n_tiles = T // TM
    n_blk = S // MOBA_BLOCK
    tiles_per_seq = S // TM
    xf = x.reshape(T, D)
    row2 = lambda v: v.reshape(1, -1)
    full = lambda shape: pl.BlockSpec(shape, lambda *_: (0,) * len(shape))

    cos_t, sin_t = _rope_tables(S)
    wk = w_in[:, A:2 * A].astype(BF16)
    wqvt = jnp.concatenate([w_in[:, :A], w_in[:, 2 * A:3 * A]], axis=1).T.astype(BF16)
    wuv = w_in[:, 3 * A:].astype(BF16)

    qt, k, vt, kmean, gm = pl.pallas_call(
        _proj_kernel,
        grid=(n_tiles,),
        in_specs=[pl.BlockSpec((TM, D), lambda i: (i, 0)), full((1, D)), full((D, A)), full((2 * A, D)),
                  full((D, 2 * GMLP_WIDTH)),
                  pl.BlockSpec((TM, A), lambda i: (i % tiles_per_seq, 0)),
                  pl.BlockSpec((TM, A), lambda i: (i % tiles_per_seq, 0)),
                  pl.BlockSpec((A, TM), lambda i: (0, i % tiles_per_seq)),
                  pl.BlockSpec((A, TM), lambda i: (0, i % tiles_per_seq)),
                  full((1, GMLP_WIDTH)), full((1, GMLP_WIDTH)),
                  full((GMLP_GROUPS, GMLP_CHUNK, GMLP_CHUNK)), full((GMLP_GROUPS, GMLP_CHUNK, 1)),
                  full((1, GMLP_WIDTH))],
        out_specs=[pl.BlockSpec((A, TM), lambda i: (0, i)), pl.BlockSpec((TM, A), lambda i: (i, 0)),
                   pl.BlockSpec((A, TM), lambda i: (0, i)),
                   pl.BlockSpec((1, TM // MOBA_BLOCK, A), lambda i: (i, 0, 0)),
                   pl.BlockSpec((TM, GMLP_WIDTH), lambda i: (i, 0))],
        out_shape=[jax.ShapeDtypeStruct((A, T), BF16), jax.ShapeDtypeStruct((T, A), BF16),
                   jax.ShapeDtypeStruct((A, T), BF16),
                   jax.ShapeDtypeStruct((n_tiles, TM // MOBA_BLOCK, A), F32),
                   jax.ShapeDtypeStruct((T, GMLP_WIDTH), BF16)],
        scratch_shapes=[pltpu.VMEM((TM, GMLP_WIDTH), F32)],
        compiler_params=_params(("parallel",)),
        name="proj",
    )(xf, row2(mix_norm), wk, wqvt, wuv, cos_t, sin_t, cos_t.T, sin_t.T, row2(gmlp_ln_g), row2(gmlp_ln_b), gmlp_ws,
      gmlp_bs.reshape(GMLP_GROUPS, GMLP_CHUNK, 1), row2(gmlp_out_norm))

    kmean = kmean.reshape(B, n_blk, A)
    n_pairs = A // LANES
    attn_t = pl.pallas_call(
        _attn_kernel,
        grid=(B, n_pairs),
        in_specs=[pl.BlockSpec((LANES, S), lambda b, h: (h, b)), pl.BlockSpec((S, LANES), lambda b, h: (b, h)),
                  pl.BlockSpec((LANES, S), lambda b, h: (h, b)),
                  pl.BlockSpec((1, n_blk, LANES), lambda b, h: (b, 0, h))],
        out_specs=pl.BlockSpec((n_blk, LANES, MOBA_BLOCK), lambda b, h: (b, h, 0)),
        out_shape=jax.ShapeDtypeStruct((T // MOBA_BLOCK, A, MOBA_BLOCK), F32),
        compiler_params=_params(("parallel", "parallel")),
        name="attn",
    )(qt, k, vt, kmean)

    x1, xs, lpos, gates, tab, tot = pl.pallas_call(
        _mix_kernel,
        grid=(n_tiles,),
        in_specs=[pl.BlockSpec((TM, D), lambda i: (i, 0)),
                  pl.BlockSpec((TM // MOBA_BLOCK, A, MOBA_BLOCK), lambda i: (i, 0, 0)),
                  pl.BlockSpec((TM, GMLP_WIDTH), lambda i: (i, 0)),
                  full((A, 1)), full((A, D)), full((GMLP_WIDTH, D)), full((1, D)),
                  full((N_EXPERTS, D)), full((N_EXPERTS, 1))],
        out_specs=[pl.BlockSpec((TM, D), lambda i: (i, 0)), pl.BlockSpec((SORT_ROWS, D), lambda i: (i, 0)),
                   pl.BlockSpec((TOP_K, TM), lambda i: (0, i)), pl.BlockSpec((TOP_K, TM), lambda i: (0, i)),
                   pl.BlockSpec((1, N_EXPERTS, LANES), lambda i: (i, 0, 0)), full((N_EXPERTS, LANES))],
        out_shape=[jax.ShapeDtypeStruct((T, D), F32), jax.ShapeDtypeStruct((n_tiles * SORT_ROWS, D), BF16),
                   jax.ShapeDtypeStruct((TOP_K, T), I32), jax.ShapeDtypeStruct((TOP_K, T), F32),
                   jax.ShapeDtypeStruct((n_tiles, N_EXPERTS, LANES), I32),
                   jax.ShapeDtypeStruct((N_EXPERTS, LANES), I32)],
        scratch_shapes=[pltpu.VMEM((N_EXPERTS, LANES), F32)],
        compiler_params=_params(("arbitrary",)),
        name="mix",
    )(xf, attn_t, gm, attn_out_norm.reshape(A, 1), w_out[:A].astype(BF16), w_out[A:].astype(BF16),
      row2(ffn_norm), router_w.T, router_b.reshape(N_EXPERTS, 1))

    n_blocks = -(-(T * TOP_K + n_tiles * N_EXPERTS * (CHUNK - 1) + N_EXPERTS * (RB - CHUNK)) // RB)
    seg16, off16, before16 = tab[:, :, 0], tab[:, :, 1], tab[:, :, 2]
    tot16 = tot[:, 0]
    blocks_e = (tot16 + CHUNKS_PER_BLOCK - 1) // CHUNKS_PER_BLOCK
    bend = jnp.cumsum(blocks_e)
    nused = bend[-1:].astype(I32)
    bidx = jnp.arange(n_blocks, dtype=I32)
    blk_e = jnp.minimum(jnp.sum(bend[None, :] <= bidx[:, None], axis=1), N_EXPERTS - 1).astype(I32)
    sel = blk_e[:, None] == jnp.arange(N_EXPERTS, dtype=I32)[None, :]
    pick = lambda v: jnp.sum(jnp.where(sel, v[None, :], 0), axis=1)
    chunk_pos = ((bidx - pick(bend - blocks_e)) * CHUNKS_PER_BLOCK)[:, None] \
        + jnp.arange(CHUNKS_PER_BLOCK, dtype=I32)[None, :]
    valid = (chunk_pos < pick(tot16)[:, None]) & (bidx < nused)[:, None]
    seg_end = jnp.sum(jnp.where(sel[:, None, :], (before16 + seg16)[None], 0), axis=2)
    shift = jnp.sum(jnp.where(sel[:, None, :], (off16 - before16)[None], 0), axis=2)
    tile = jnp.minimum(jnp.sum(seg_end[:, None, :] <= chunk_pos[:, :, None], axis=2), n_tiles - 1)
    in_tile = tile[:, :, None] == jnp.arange(n_tiles, dtype=I32)[None, None, :]
    where16 = tile * CHUNKS_PER_TILE + jnp.sum(jnp.where(in_tile, shift[:, None, :], 0), axis=2) + chunk_pos
    dump16 = (n_tiles * CHUNKS_PER_TILE + (bidx % 2)[:, None] * CHUNKS_PER_BLOCK
              + jnp.arange(CHUNKS_PER_BLOCK, dtype=I32)[None, :])
    tabs = jnp.concatenate([jnp.where(valid, where16, 0), jnp.where(valid, where16, dump16)],
                           axis=1).astype(I32).reshape(n_blocks, 1, 2 * CHUNKS_PER_BLOCK)

    last = lambda i, nu: jnp.minimum(i, nu[0] - 1)
    smem_tab = lambda step: pl.BlockSpec((1, 1, 2 * CHUNKS_PER_BLOCK),
                                         lambda i, be, nu: (jnp.minimum(i + step, n_blocks - 1), 0, 0),
                                         memory_space=pltpu.SMEM)
    y = pl.pallas_call(
        _expert_kernel,
        grid_spec=pltpu.PrefetchScalarGridSpec(
            num_scalar_prefetch=2, grid=(n_blocks,),
            in_specs=[smem_tab(0), smem_tab(1), pl.BlockSpec(memory_space=pl.ANY),
                      pl.BlockSpec((1, D, 2 * D_EXPERT), lambda i, be, nu: (be[last(i, nu)], 0, 0)),
                      pl.BlockSpec((1, 1, 2 * D_EXPERT), lambda i, be, nu: (be[last(i, nu)], 0, 0)),
                      pl.BlockSpec((1, D_EXPERT, D), lambda i, be, nu: (be[last(i, nu)], 0, 0)),
                      pl.BlockSpec((1, 1, D), lambda i, be, nu: (be[last(i, nu)], 0, 0))],
            out_specs=pl.BlockSpec(memory_space=pl.ANY),
            scratch_shapes=[pltpu.VMEM((2, RB, D), BF16), pltpu.VMEM((2, RB, D), BF16),
                            pltpu.VMEM((SORT_ROWS - PAIRS, D), BF16),
                            pltpu.VMEM((D, 2 * D_EXPERT), BF16), pltpu.VMEM((D_EXPERT, D), BF16),
                            pltpu.SemaphoreType.DMA((2,)), pltpu.SemaphoreType.DMA((2,)),
                            pltpu.SemaphoreType.DMA(())]),
        out_shape=jax.ShapeDtypeStruct((n_tiles * SORT_ROWS + DUMP_ROWS, D), BF16),
        compiler_params=pltpu.CompilerParams(dimension_semantics=("arbitrary",),
                                             vmem_limit_bytes=EXPERT_VMEM_LIMIT),
        name="experts",
    )(blk_e, nused, tabs, tabs, xs, w_gate_up, b_gate_up.reshape(N_EXPERTS, 1, -1), w_down,
      b_down.reshape(N_EXPERTS, 1, -1))

    out = pl.pallas_call(
        _final_kernel,
        grid=(n_tiles,),
        in_specs=[pl.BlockSpec((SORT_ROWS, D), lambda i: (i, 0)),
                  pl.BlockSpec((TM, D), lambda i: (i, 0)), pl.BlockSpec((TM, TOP_K), lambda i: (i, 0)),
                  pl.BlockSpec((TM, TOP_K), lambda i: (i, 0)),
                  pl.BlockSpec((TM, PLE_DIM), lambda i: (i, 0)), full((1, D)), full((D, D)), full((PLE_DIM, D)),
                  full((1, D)), full((1, D))],
        out_specs=pl.BlockSpec((TM, D), lambda i: (i, 0)),
        out_shape=jax.ShapeDtypeStruct((T, D), F32),
        compiler_params=_params(("parallel",)),
        name="final",
    )(y, x1, lpos.T, gates.T, p.reshape(T, PLE_DIM), row2(ple_norm), w_ple_gate.astype(BF16),
      w_ple_proj.astype(BF16), row2(ple_post_norm), row2(out_norm))
    return out.reshape(B, S, D)


def kernel(x, p, mix_norm, w_in, gmlp_ln_g, gmlp_ln_b, gmlp_ws, gmlp_bs, attn_out_norm, gmlp_out_norm, w_out, ffn_norm, router_w, router_b, w_gate_up, b_gate_up, w_down, b_down, ple_norm, w_ple_gate, w_ple_proj, ple_post_norm, final_norm):
    depth = p.shape[0]
    assert depth == 1, "the final rmsnorm is fused into the (single) layer"
    return _layer(x, p[0], mix_norm[0], w_in[0], gmlp_ln_g[0], gmlp_ln_b[0], gmlp_ws[0], gmlp_bs[0],
                  attn_out_norm[0], gmlp_out_norm[0], w_out[0], ffn_norm[0], router_w[0], router_b[0],
                  w_gate_up[0], b_gate_up[0], w_down[0], b_down[0], ple_norm[0], w_ple_gate[0],
                  w_ple_proj[0], ple_post_norm[0], final_norm)
```

```python
import jax
import jax.numpy as jnp
import numpy as np
from jax import lax
from jax.experimental import pallas as pl
from jax.experimental.pallas import tpu as pltpu

F32 = jnp.float32
BF16 = jnp.bfloat16
I32 = jnp.int32

D_MODEL = 1024
PLE_DIM = 256
ATTN_WIDTH = 512
HEAD_DIM = 64
ROPE_THETA = 10000.0
MOBA_BLOCK = 256
MOBA_TOPK = 3
GMLP_WIDTH = 512
GMLP_GROUPS = 4
GMLP_GROUP_DIM = 128
GMLP_CHUNK = 128
N_EXPERTS = 32
TOP_K = 4
D_EXPERT = 1024
SWIGLU_LIMIT = 7.0
SWIGLU_ALPHA = 1.702
NORM_EPS = 1e-6
NEG_INF = -1e30
Q_SCALE = float(np.log2(np.e) / np.sqrt(HEAD_DIM))

LANES = 128
CHUNK = 16

TM = 512
RB = 512
FC = 512
PAIRS = TM * TOP_K
SORT_ROWS = 2560
CHUNKS_PER_BLOCK = RB // CHUNK
CHUNKS_PER_TILE = SORT_ROWS // CHUNK
DUMP_ROWS = 2 * RB
VMEM_LIMIT = 48 * 1024 * 1024
EXPERT_VMEM_LIMIT = 56 * 1024 * 1024

assert SORT_ROWS % TM == 0 and SORT_ROWS >= PAIRS + N_EXPERTS * (CHUNK - 1)


def _rms(x, g):
    return x * lax.rsqrt(jnp.mean(x * x, axis=-1, keepdims=True) + NORM_EPS) * g


def _nt(a, b):
    return lax.dot_general(a, b, (((1,), (1,)), ((), ())), preferred_element_type=F32)


def _proj_kernel(x_ref, mixn_ref, wk_ref, wqvt_ref, wuv_ref, cos_ref, sin_ref, cost_ref, sint_ref, lng_ref,
                 lnb_ref, ws_ref, bs_ref, gon_ref, qt_ref, k_ref, vt_ref, kmean_ref, gm_ref, gacc_ref):
    h = _rms(x_ref[...], mixn_ref[...]).astype(BF16)
    half = HEAD_DIM // 2

    uv = jnp.dot(h, wuv_ref[...], preferred_element_type=F32)
    kx = jnp.dot(h, wk_ref[...], preferred_element_type=F32)
    qvt = _nt(wqvt_ref[...], h)
    inv_sqrt2 = np.float32(1.0 / np.sqrt(2.0))

    def gelu(t):
        return 0.5 * t * (1.0 + lax.erf(t * inv_sqrt2))

    gu = gelu(uv[:, :GMLP_WIDTH])
    gv = gelu(uv[:, GMLP_WIDTH:])
    row = lax.broadcasted_iota(I32, (GMLP_CHUNK, GMLP_CHUNK), 0)
    col = lax.broadcasted_iota(I32, (GMLP_CHUNK, GMLP_CHUNK), 1)
    tril = col <= row
    for g in range(GMLP_GROUPS):
        sl = slice(g * GMLP_GROUP_DIM, (g + 1) * GMLP_GROUP_DIM)
        vg = gv[:, sl]
        mu = jnp.mean(vg, axis=-1, keepdims=True)
        dv = vg - mu
        var = jnp.mean(dv * dv, axis=-1, keepdims=True)
        vn = (dv * lax.rsqrt(var + NORM_EPS) * lng_ref[:, sl] + lnb_ref[:, sl]).astype(BF16)
        w = jnp.where(tril, ws_ref[g], 0.0).astype(BF16)
        bias = bs_ref[g]
        for c in range(TM // GMLP_CHUNK):
            rs = slice(c * GMLP_CHUNK, (c + 1) * GMLP_CHUNK)
            mixed = jnp.dot(w, vn[rs], preferred_element_type=F32) + bias
            gacc_ref[rs, sl] = gu[rs, sl] * mixed
    gm_ref[...] = _rms(gacc_ref[...], gon_ref[...]).astype(BF16)

    lane = lax.broadcasted_iota(I32, (TM, LANES), 1)
    first_half = (lane % HEAD_DIM) < half
    parts = []
    for s in range(ATTN_WIDTH // LANES):
        slab = kx[:, s * LANES:(s + 1) * LANES]
        ahead = pltpu.roll(slab, LANES - half, 1)
        behind = pltpu.roll(slab, half, 1)
        parts.append(jnp.where(first_half, ahead, behind))
    k = kx * cos_ref[...] + jnp.concatenate(parts, axis=1) * sin_ref[...]
    k_ref[...] = k.astype(BF16)
    for c in range(TM // MOBA_BLOCK):
        kmean_ref[0, c:c + 1, :] = jnp.mean(k[c * MOBA_BLOCK:(c + 1) * MOBA_BLOCK], axis=0, keepdims=True)

    qx = qvt[:ATTN_WIDTH]
    swapped = []
    for hd in range(ATTN_WIDTH // HEAD_DIM):
        r0 = hd * HEAD_DIM
        swapped += [qx[r0 + half:r0 + HEAD_DIM], qx[r0:r0 + half]]
    qt = (qx * cost_ref[...] + jnp.concatenate(swapped, axis=0) * sint_ref[...]) * Q_SCALE
    qt_ref[...] = qt.astype(BF16)
    vt_ref[...] = qvt[ATTN_WIDTH:].astype(BF16)


def _attn_kernel(qt_ref, k_ref, vt_ref, kmean_ref, o_ref):
    seq = k_ref.shape[0]
    n_blk = seq // MOBA_BLOCK
    feat = lax.broadcasted_iota(I32, (LANES, 1), 0)
    lane = lax.broadcasted_iota(I32, (1, LANES), 1)
    blk_of_key = lax.broadcasted_iota(I32, (n_blk, seq), 0)
    blk_of_query = lax.broadcasted_iota(I32, (n_blk, seq), 1) // MOBA_BLOCK
    key_pos = lax.broadcasted_iota(I32, (MOBA_BLOCK, MOBA_BLOCK), 0)
    query_pos = lax.broadcasted_iota(I32, (MOBA_BLOCK, MOBA_BLOCK), 1)
    causal = key_pos <= query_pos
    blk = lambda i: slice(i * MOBA_BLOCK, (i + 1) * MOBA_BLOCK)
    ones = jnp.ones((16, seq), BF16)

    n_heads = LANES // HEAD_DIM
    dotf = lambda a, b: jnp.dot(a, b, preferred_element_type=F32)
    qts, biases = [], []
    for h in range(n_heads):
        rows_in_head = (feat >= h * HEAD_DIM) & (feat < (h + 1) * HEAD_DIM)
        qt = jnp.where(rows_in_head, qt_ref[...], jnp.zeros((), BF16))

        km = jnp.where((lane >= h * HEAD_DIM) & (lane < (h + 1) * HEAD_DIM), kmean_ref[0], 0.0)
        km1 = km.astype(BF16)
        r1 = km - km1.astype(F32)
        km2 = r1.astype(BF16)
        km3 = (r1 - km2.astype(F32)).astype(BF16)
        gate = dotf(km1, qt) + dotf(km2, qt) + dotf(km3, qt)
        rank = jnp.zeros((n_blk, seq), F32)
        for m in range(n_blk):
            gm_ = gate[m:m + 1, :]
            beats = (m < blk_of_query) & ((gm_ > gate) | ((gm_ == gate) & (m < blk_of_key)))
            rank = rank + jnp.where(beats, 1.0, 0.0)
        chosen = (blk_of_key < blk_of_query) & (rank < MOBA_TOPK)
        qts.append(qt)
        biases.append(jnp.where(chosen, 0.0, NEG_INF))

    def scores(i):
        return dotf(k_ref[0:(i + 1) * MOBA_BLOCK, :], jnp.concatenate([qt[:, blk(i)] for qt in qts], axis=1))

    def finish(i, h, pu):
        hs = slice(h * HEAD_DIM, (h + 1) * HEAD_DIM)
        n_keys = (i + 1) * MOBA_BLOCK
        ov = dotf(jnp.concatenate([vt_ref[hs, 0:n_keys], ones[:, 0:n_keys]], axis=0), pu)
        o_ref[i, hs, :] = ov[0:HEAD_DIM] * (1.0 / ov[HEAD_DIM:HEAD_DIM + 1])

    order = list(range(n_blk - 1, -1, -1))
    s_next = scores(order[0])
    unfinished = []
    for pos, i in enumerate(order):
        s_pair = s_next
        if pos + 1 < n_blk:
            s_next = scores(order[pos + 1])
        probs = []
        for h in range(n_heads):
            s = s_pair[:, blk(h)]
            bias = biases[h]
            tiles = [s[blk(j)] + bias[j:j + 1, blk(i)] for j in range(i)]
            tiles.append(jnp.where(causal, s[blk(i)], NEG_INF))
            top = tiles[0]
            for t in tiles[1:]:
                top = jnp.maximum(top, t)
            m_q = jnp.max(top, axis=0, keepdims=True)
            probs.append(jnp.concatenate([jnp.exp2(t - m_q).astype(BF16) for t in tiles], axis=0))
        for args in unfinished:
            finish(*args)
        unfinished = [(i, h, probs[h]) for h in range(n_heads)]
    for args in unfinished:
        finish(*args)


def _mix_kernel(x_ref, at_ref, gm_ref, aon_ref, woa_ref, wog_ref, ffn_ref, rwt_ref, rb_ref,
                x1_ref, xs_ref, lpos_ref, gate_ref, tab_ref, tot_ref, carry_ref):
    @pl.when(pl.program_id(0) == 0)
    def _():
        carry_ref[...] = jnp.zeros_like(carry_ref)

    parts = []
    for c in range(TM // MOBA_BLOCK):
        at = at_ref[c]
        ss = jnp.sum(at * at, axis=0, keepdims=True)
        atn = at * lax.rsqrt(ss * (1.0 / ATTN_WIDTH) + NORM_EPS) * aon_ref[...]
        parts.append(atn.T)
    attn = jnp.concatenate(parts, axis=0).astype(BF16)
    x1 = (x_ref[...] + jnp.dot(attn, woa_ref[...], preferred_element_type=F32)
          + jnp.dot(gm_ref[...], wog_ref[...], preferred_element_type=F32))
    x1_ref[...] = x1
    hn = _rms(x1, ffn_ref[...])

    hn_b = hn.astype(BF16)
    hn_lo = (hn - hn_b.astype(F32)).astype(BF16)
    rw = rwt_ref[...]
    rw_hi = rw.astype(BF16)
    rw_lo = (rw - rw_hi.astype(F32)).astype(BF16)
    both = _nt(jnp.concatenate([rw_hi, rw_lo], axis=0), hn_b)
    logits = both[:N_EXPERTS] + both[N_EXPERTS:] + _nt(rw_hi, hn_lo) + rb_ref[...]
    eidx = lax.broadcasted_iota(I32, (N_EXPERTS, TM), 0)
    vals, idxs = [], []
    rest = logits
    for _ in range(TOP_K):
        m = jnp.max(rest, axis=0, keepdims=True)
        idx = jnp.min(jnp.where(rest == m, eidx, N_EXPERTS), axis=0, keepdims=True)
        vals.append(m)
        idxs.append(idx)
        rest = jnp.where(eidx == idx, -jnp.inf, rest)
    exps = [jnp.exp(v - vals[0]) for v in vals]
    denom = exps[0] + exps[1] + exps[2] + exps[3]
    hot = [eidx == idx for idx in idxs]
    multi = jnp.where(hot[0] | hot[1] | hot[2] | hot[3], 1.0, 0.0)

    tp = lax.broadcasted_iota(I32, (TM, TM), 0)
    tq = lax.broadcasted_iota(I32, (TM, TM), 1)
    earlier = jnp.where(tp < tq, 1.0, 0.0).astype(BF16)
    before = jnp.dot(multi.astype(BF16), earlier, preferred_element_type=F32)
    count = jnp.sum(multi, axis=1, keepdims=True)
    seg = jnp.floor((count + (CHUNK - 1)) * (1.0 / CHUNK))
    ep = lax.broadcasted_iota(I32, (N_EXPERTS, N_EXPERTS), 0)
    eq = lax.broadcasted_iota(I32, (N_EXPERTS, N_EXPERTS), 1)
    lower = jnp.where(eq < ep, 1.0, 0.0).astype(BF16)
    seg_wide = jnp.broadcast_to(seg, (N_EXPERTS, LANES))
    off = jnp.dot(lower, seg_wide.astype(BF16), preferred_element_type=F32)
    slot = off[:, 0:1] * CHUNK + before
    lpos = [jnp.sum(jnp.where(hot[kk], slot, 0.0), axis=0, keepdims=True).astype(I32) for kk in range(TOP_K)]
    for kk in range(TOP_K):
        lpos_ref[kk:kk + 1, :] = lpos[kk]
        gate_ref[kk:kk + 1, :] = exps[kk] / denom
    def onehot_rows(c):
        r = lax.broadcasted_iota(I32, (TM, TM), 0) + c * TM
        onehot = jnp.zeros((TM, TM), F32)
        for kk in range(TOP_K):
            onehot = jnp.where(r == lpos[kk], 1.0, onehot)
        return onehot.astype(BF16)

    n_chunks = SORT_ROWS // TM
    nxt = onehot_rows(0)
    for c in range(n_chunks):
        cur = nxt
        if c + 1 < n_chunks:
            nxt = onehot_rows(c + 1)
        xs_ref[c * TM:(c + 1) * TM, :] = jnp.dot(cur, hn_b, preferred_element_type=F32).astype(BF16)
    lane = lax.broadcasted_iota(I32, (N_EXPERTS, LANES), 1)
    tab = jnp.where(lane == 0, seg_wide, jnp.where(lane == 1, off, carry_ref[...]))
    tab_ref[0] = tab.astype(I32)
    carry_ref[...] = carry_ref[...] + seg
    tot_ref[...] = carry_ref[...].astype(I32)


def _chunk_rows(ref, chunk_index):
    return ref.at[pl.ds(pl.multiple_of(chunk_index * CHUNK, CHUNK), CHUNK), :]


def _expert_kernel(blke_ref, nused_ref, tab_ref, tab_next_ref, xs_ref, wgu_ref, bgu_ref, wdn_ref, bdn_ref, y_ref,
                   xbuf, ybuf, zbuf, wgu_bf, wdn_bf, sem_in, sem_out, sem_fill):
    i = pl.program_id(0)
    n_used = nused_ref[0]
    used = i < n_used
    slot = i % 2

    def copies_in(tab, s):
        return [pltpu.make_async_copy(_chunk_rows(xs_ref, tab[0, 0, c]), _chunk_rows(xbuf.at[s], c), sem_in.at[s])
                for c in range(CHUNKS_PER_BLOCK)]

    def copies_out(s):
        return [pltpu.make_async_copy(_chunk_rows(ybuf.at[s], c), _chunk_rows(y_ref, tab_ref[0, 0, CHUNKS_PER_BLOCK + c]),
                                      sem_out.at[s]) for c in range(CHUNKS_PER_BLOCK)]

    @pl.when(i == 0)
    def _():
        zbuf[...] = jnp.zeros_like(zbuf)
        n_regions = y_ref.shape[0] // SORT_ROWS
        tail = SORT_ROWS - PAIRS
        starts = [t * SORT_ROWS + PAIRS for t in range(n_regions)]
        starts += [n_regions * SORT_ROWS + d * tail for d in range(DUMP_ROWS // tail)]
        fills = [pltpu.make_async_copy(zbuf, y_ref.at[pl.ds(r0, tail), :], sem_fill) for r0 in starts]
        for f in fills:
            f.start()
        for f in fills:
            f.wait()
        for cp in copies_in(tab_ref, 0):
            cp.start()

    @pl.when(used & ((i == 0) | (blke_ref[i] != blke_ref[jnp.maximum(i - 1, 0)])))
    def _():
        for c in range(2 * D_EXPERT // FC):
            wgu_bf[:, c * FC:(c + 1) * FC] = wgu_ref[0, :, c * FC:(c + 1) * FC].astype(BF16)
        for c in range(D_EXPERT // FC):
            wdn_bf[c * FC:(c + 1) * FC, :] = wdn_ref[0, c * FC:(c + 1) * FC, :].astype(BF16)

    @pl.when(used)
    def _():
        for cp in copies_in(tab_ref, slot):
            cp.wait()

        @pl.when(i >= 2)
        def _():
            for cp in copies_out(slot):
                cp.wait()

        for cp in copies_in(tab_next_ref, 1 - slot):
            cp.start()
        x = xbuf[slot]
        y = jnp.zeros((RB, D_MODEL), F32) + bdn_ref[0]
        for c in range(D_EXPERT // FC):
            g = jnp.dot(x, wgu_bf[:, c * FC:(c + 1) * FC], preferred_element_type=F32) \
                + bgu_ref[0, :, c * FC:(c + 1) * FC]
            lin = jnp.dot(x, wgu_bf[:, D_EXPERT + c * FC:D_EXPERT + (c + 1) * FC],
                          preferred_element_type=F32) + bgu_ref[0, :, D_EXPERT + c * FC:D_EXPERT + (c + 1) * FC]
            g = jnp.minimum(g, SWIGLU_LIMIT)
            lin = jnp.clip(lin, -SWIGLU_LIMIT, SWIGLU_LIMIT)
            a = g * (1.0 / (1.0 + jnp.exp(-SWIGLU_ALPHA * g))) * (lin + 1.0)
            y = y + jnp.dot(a.astype(BF16), wdn_bf[c * FC:(c + 1) * FC, :], preferred_element_type=F32)
        ybuf[slot] = y.astype(BF16)
        for cp in copies_out(slot):
            cp.start()

        @pl.when(i == n_used - 1)
        def _():
            for cp in copies_in(tab_next_ref, 1 - slot):
                cp.wait()
            for cp in copies_out(slot):
                cp.wait()

            @pl.when(i >= 1)
            def _():
                for cp in copies_out(1 - slot):
                    cp.wait()


def _final_kernel(y_ref, x1_ref, lpos_ref, gate_ref, p_ref, plen_ref, wg_ref, wp_ref, post_ref, fin_ref, o_ref):
    lpos = lpos_ref[...]
    gates = gate_ref[...]
    def gate_cols(c):
        r = lax.broadcasted_iota(I32, (TM, TM), 1) + c * TM
        w = jnp.zeros((TM, TM), F32)
        for kk in range(TOP_K):
            w = jnp.where(r == lpos[:, kk:kk + 1], gates[:, kk:kk + 1], w)
        return w.astype(BF16)

    emb = jnp.dot(p_ref[...].astype(BF16), wp_ref[...], preferred_element_type=F32)
    n_chunks = SORT_ROWS // TM
    moe = jnp.zeros((TM, D_MODEL), F32)
    nxt = gate_cols(0)
    for c in range(n_chunks):
        cur = nxt
        if c + 1 < n_chunks:
            nxt = gate_cols(c + 1)
        moe = moe + jnp.dot(cur, y_ref[c * TM:(c + 1) * TM, :], preferred_element_type=F32)
    x2 = x1_ref[...] + moe
    z = jnp.dot(_rms(x2, plen_ref[...]).astype(BF16), wg_ref[...], preferred_element_type=F32)
    ple_gate = 1.0 / (1.0 + jnp.exp(-z))
    x3 = x2 + ple_gate * _rms(emb, post_ref[...])
    o_ref[...] = _rms(x3, fin_ref[...])


def _params(sem):
    return pltpu.CompilerParams(dimension_semantics=sem, vmem_limit_bytes=VMEM_LIMIT)


def _rope_tables(seq):
    half = HEAD_DIM // 2
    inv = ROPE_THETA ** (-jnp.arange(half, dtype=F32) / half)
    ang = jnp.arange(seq, dtype=F32)[:, None] * inv[None, :]
    cos = jnp.cos(ang)
    sin = jnp.sin(ang)
    n_heads = ATTN_WIDTH // HEAD_DIM
    cos_t = jnp.tile(jnp.concatenate([cos, cos], axis=1), (1, n_heads))
    sin_t = jnp.tile(jnp.concatenate([-sin, sin], axis=1), (1, n_heads))
    return cos_t, sin_t


def _layer(x, p, mix_norm, w_in, gmlp_ln_g, gmlp_ln_b, gmlp_ws, gmlp_bs, attn_out_norm, gmlp_out_norm,
           w_out, ffn_norm, router_w, router_b, w_gate_up, b_gate_up, w_down, b_down, ple_norm,
           w_ple_gate, w_ple_proj, ple_post_norm, out_norm):
    B, S, D = x.shape
    T = B * S
    A = ATTN_WIDTH
    n_tiles = T // TM
    n_blk = S // MOBA_BLOCK
    tiles_per_seq = S // TM
    xf = x.reshape(T, D)
    row2 = lambda v: v.reshape(1, -1)
    full = lambda shape: pl.BlockSpec(shape, lambda *_: (0,) * len(shape))

    cos_t, sin_t = _rope_tables(S)
    wk = w_in[:, A:2 * A].astype(BF16)
    wqvt = jnp.concatenate([w_in[:, :A], w_in[:, 2 * A:3 * A]], axis=1).T.astype(BF16)
    wuv = w_in[:, 3 * A:].astype(BF16)

    qt, k, vt, kmean, gm = pl.pallas_call(
        _proj_kernel,
        grid=(n_tiles,),
        in_specs=[pl.BlockSpec((TM, D), lambda i: (i, 0)), full((1, D)), full((D, A)), full((2 * A, D)),
                  full((D, 2 * GMLP_WIDTH)),
                  pl.BlockSpec((TM, A), lambda i: (i % tiles_per_seq, 0)),
                  pl.BlockSpec((TM, A), lambda i: (i % tiles_per_seq, 0)),
                  pl.BlockSpec((A, TM), lambda i: (0, i % tiles_per_seq)),
                  pl.BlockSpec((A, TM), lambda i: (0, i % tiles_per_seq)),
                  full((1, GMLP_WIDTH)), full((1, GMLP_WIDTH)),
                  full((GMLP_GROUPS, GMLP_CHUNK, GMLP_CHUNK)), full((GMLP_GROUPS, GMLP_CHUNK, 1)),
                  full((1, GMLP_WIDTH))],
        out_specs=[pl.BlockSpec((A, TM), lambda i: (0, i)), pl.BlockSpec((TM, A), lambda i: (i, 0)),
                   pl.BlockSpec((A, TM), lambda i: (0, i)),
                   pl.BlockSpec((1, TM // MOBA_BLOCK, A), lambda i: (i, 0, 0)),
                   pl.BlockSpec((TM, GMLP_WIDTH), lambda i: (i, 0))],
        out_shape=[jax.ShapeDtypeStruct((A, T), BF16), jax.ShapeDtypeStruct((T, A), BF16),
                   jax.ShapeDtypeStruct((A, T), BF16),
                   jax.ShapeDtypeStruct((n_tiles, TM // MOBA_BLOCK, A), F32),
                   jax.ShapeDtypeStruct((T, GMLP_WIDTH), BF16)],
        scratch_shapes=[pltpu.VMEM((TM, GMLP_WIDTH), F32)],
        compiler_params=_params(("parallel",)),
        name="proj",
    )(xf, row2(mix_norm), wk, wqvt, wuv, cos_t, sin_t, cos_t.T, sin_t.T, row2(gmlp_ln_g), row2(gmlp_ln_b), gmlp_ws,
      gmlp_bs.reshape(GMLP_GROUPS, GMLP_CHUNK, 1), row2(gmlp_out_norm))

    kmean = kmean.reshape(B, n_blk, A)
    n_pairs = A // LANES
    attn_t = pl.pallas_call(
        _attn_kernel,
        grid=(B, n_pairs),
        in_specs=[pl.BlockSpec((LANES, S), lambda b, h: (h, b)), pl.BlockSpec((S, LANES), lambda b, h: (b, h)),
                  pl.BlockSpec((LANES, S), lambda b, h: (h, b)),
                  pl.BlockSpec((1, n_blk, LANES), lambda b, h: (b, 0, h))],
        out_specs=pl.BlockSpec((n_blk, LANES, MOBA_BLOCK), lambda b, h: (b, h, 0)),
        out_shape=jax.ShapeDtypeStruct((T // MOBA_BLOCK, A, MOBA_BLOCK), F32),
        compiler_params=_params(("parallel", "parallel")),
        name="attn",
    )(qt, k, vt, kmean)

    x1, xs, lpos, gates, tab, tot = pl.pallas_call(
        _mix_kernel,
        grid=(n_tiles,),
        in_specs=[pl.BlockSpec((TM, D), lambda i: (i, 0)),
                  pl.BlockSpec((TM // MOBA_BLOCK, A, MOBA_BLOCK), lambda i: (i, 0, 0)),
                  pl.BlockSpec((TM, GMLP_WIDTH), lambda i: (i, 0)),
                  full((A, 1)), full((A, D)), full((GMLP_WIDTH, D)), full((1, D)),
                  full((N_EXPERTS, D)), full((N_EXPERTS, 1))],
        out_specs=[pl.BlockSpec((TM, D), lambda i: (i, 0)), pl.BlockSpec((SORT_ROWS, D), lambda i: (i, 0)),
                   pl.BlockSpec((TOP_K, TM), lambda i: (0, i)), pl.BlockSpec((TOP_K, TM), lambda i: (0, i)),
                   pl.BlockSpec((1, N_EXPERTS, LANES), lambda i: (i, 0, 0)), full((N_EXPERTS, LANES))],
        out_shape=[jax.ShapeDtypeStruct((T, D), F32), jax.ShapeDtypeStruct((n_tiles * SORT_ROWS, D), BF16),
                   jax.ShapeDtypeStruct((TOP_K, T), I32), jax.ShapeDtypeStruct((TOP_K, T), F32),
                   jax.ShapeDtypeStruct((n_tiles, N_EXPERTS, LANES), I32),
                   jax.ShapeDtypeStruct((N_EXPERTS, LANES), I32)],
        scratch_shapes=[pltpu.VMEM((N_EXPERTS, LANES), F32)],
        compiler_params=_params(("arbitrary",)),
        name="mix",
    )(xf, attn_t, gm, attn_out_norm.reshape(A, 1), w_out[:A].astype(BF16), w_out[A:].astype(BF16),
      row2(ffn_norm), router_w.T, router_b.reshape(N_EXPERTS, 1))

    n_blocks = -(-(T * TOP_K + n_tiles * N_EXPERTS * (CHUNK - 1) + N_EXPERTS * (RB - CHUNK)) // RB)
    seg16, off16, before16 = tab[:, :, 0], tab[:, :, 1], tab[:, :, 2]
    tot16 = tot[:, 0]
    blocks_e = (tot16 + CHUNKS_PER_BLOCK - 1) // CHUNKS_PER_BLOCK
    bend = jnp.cumsum(blocks_e)
    nused = bend[-1:].astype(I32)
    bidx = jnp.arange(n_blocks, dtype=I32)
    blk_e = jnp.minimum(jnp.sum(bend[None, :] <= bidx[:, None], axis=1), N_EXPERTS - 1).astype(I32)
    sel = blk_e[:, None] == jnp.arange(N_EXPERTS, dtype=I32)[None, :]
    pick = lambda v: jnp.sum(jnp.where(sel, v[None, :], 0), axis=1)
    chunk_pos = ((bidx - pick(bend - blocks_e)) * CHUNKS_PER_BLOCK)[:, None] \
        + jnp.arange(CHUNKS_PER_BLOCK, dtype=I32)[None, :]
    valid = (chunk_pos < pick(tot16)[:, None]) & (bidx < nused)[:, None]
    seg_end = jnp.sum(jnp.where(sel[:, None, :], (before16 + seg16)[None], 0), axis=2)
    shift = jnp.sum(jnp.where(sel[:, None, :], (off16 - before16)[None], 0), axis=2)
    tile = jnp.minimum(jnp.sum(seg_end[:, None, :] <= chunk_pos[:, :, None], axis=2), n_tiles - 1)
    in_tile = tile[:, :, None] == jnp.arange(n_tiles, dtype=I32)[None, None, :]
    where16 = tile * CHUNKS_PER_TILE + jnp.sum(jnp.where(in_tile, shift[:, None, :], 0), axis=2) + chunk_pos
    dump16 = (n_tiles * CHUNKS_PER_TILE + (bidx % 2)[:, None] * CHUNKS_PER_BLOCK
              + jnp.arange(CHUNKS_PER_BLOCK, dtype=I32)[None, :])
    tabs = jnp.concatenate([jnp.where(valid, where16, 0), jnp.where(valid, where16, dump16)],
                           axis=1).astype(I32).reshape(n_blocks, 1, 2 * CHUNKS_PER_BLOCK)

    last = lambda i, nu: jnp.minimum(i, nu[0] - 1)
    smem_tab = lambda step: pl.BlockSpec((1, 1, 2 * CHUNKS_PER_BLOCK),
                                         lambda i, be, nu: (jnp.minimum(i + step, n_blocks - 1), 0, 0),
                                         memory_space=pltpu.SMEM)
    y = pl.pallas_call(
        _expert_kernel,
        grid_spec=pltpu.PrefetchScalarGridSpec(
            num_scalar_prefetch=2, grid=(n_blocks,),
            in_specs=[smem_tab(0), smem_tab(1), pl.BlockSpec(memory_space=pl.ANY),
                      pl.BlockSpec((1, D, 2 * D_EXPERT), lambda i, be, nu: (be[last(i, nu)], 0, 0)),
                      pl.BlockSpec((1, 1, 2 * D_EXPERT), lambda i, be, nu: (be[last(i, nu)], 0, 0)),
                      pl.BlockSpec((1, D_EXPERT, D), lambda i, be, nu: (be[last(i, nu)], 0, 0)),
                      pl.BlockSpec((1, 1, D), lambda i, be, nu: (be[last(i, nu)], 0, 0))],
            out_specs=pl.BlockSpec(memory_space=pl.ANY),
            scratch_shapes=[pltpu.VMEM((2, RB, D), BF16), pltpu.VMEM((2, RB, D), BF16),
                            pltpu.VMEM((SORT_ROWS - PAIRS, D), BF16),
                            pltpu.VMEM((D, 2 * D_EXPERT), BF16), pltpu.VMEM((D_EXPERT, D), BF16),
                            pltpu.SemaphoreType.DMA((2,)), pltpu.SemaphoreType.DMA((2,)),
                            pltpu.SemaphoreType.DMA(())]),
        out_shape=jax.ShapeDtypeStruct((n_tiles * SORT_ROWS + DUMP_ROWS, D), BF16),
        compiler_params=pltpu.CompilerParams(dimension_semantics=("arbitrary",),
                                             vmem_limit_bytes=EXPERT_VMEM_LIMIT),
        name="experts",
    )(blk_e, nused, tabs, tabs, xs, w_gate_up, b_gate_up.reshape(N_EXPERTS, 1, -1), w_down,
      b_down.reshape(N_EXPERTS, 1, -1))

    out = pl.pallas_call(
        _final_kernel,
        grid=(n_tiles,),
        in_specs=[pl.BlockSpec((SORT_ROWS, D), lambda i: (i, 0)),
                  pl.BlockSpec((TM, D), lambda i: (i, 0)), pl.BlockSpec((TM, TOP_K), lambda i: (i, 0)),
                  pl.BlockSpec((TM, TOP_K), lambda i: (i, 0)),
                  pl.BlockSpec((TM, PLE_DIM), lambda i: (i, 0)), full((1, D)), full((D, D)), full((PLE_DIM, D)),
                  full((1, D)), full((1, D))],
        out_specs=pl.BlockSpec((TM, D), lambda i: (i, 0)),
        out_shape=jax.ShapeDtypeStruct((T, D), F32),
        compiler_params=_params(("parallel",)),
        name="final",
    )(y, x1, lpos.T, gates.T, p.reshape(T, PLE_DIM), row2(ple_norm), w_ple_gate.astype(BF16),
      w_ple_proj.astype(BF16), row2(ple_post_norm), row2(out_norm))
    return out.reshape(B, S, D)


def kernel(x, p, mix_norm, w_in, gmlp_ln_g, gmlp_ln_b, gmlp_ws, gmlp_bs, attn_out_norm, gmlp_out_norm, w_out, ffn_norm, router_w, router_b, w_gate_up, b_gate_up, w_down, b_down, ple_norm, w_ple_gate, w_ple_proj, ple_post_norm, final_norm):
    depth = p.shape[0]
    assert depth == 1, "the final rmsnorm is fused into the (single) layer"
    return _layer(x, p[0], mix_norm[0], w_in[0], gmlp_ln_g[0], gmlp_ln_b[0], gmlp_ws[0], gmlp_bs[0],
                  attn_out_norm[0], gmlp_out_norm[0], w_out[0], ffn_norm[0], router_w[0], router_b[0],
                  w_gate_up[0], b_gate_up[0], w_down[0], b_down[0], ple_norm[0], w_ple_gate[0],
                  w_ple_proj[0], ple_post_norm[0], final_norm)
```

```python
import jax
import jax.numpy as jnp
import numpy as np
from jax import lax
from jax.experimental import pallas as pl
from jax.experimental.pallas import tpu as pltpu

F32 = jnp.float32
BF16 = jnp.bfloat16
I32 = jnp.int32

D_MODEL = 1024
PLE_DIM = 256
ATTN_WIDTH = 512
HEAD_DIM = 64
ROPE_THETA = 10000.0
MOBA_BLOCK = 256
MOBA_TOPK = 3
GMLP_WIDTH = 512
GMLP_GROUPS = 4
GMLP_GROUP_DIM = 128
GMLP_CHUNK = 128
N_EXPERTS = 32
TOP_K = 4
D_EXPERT = 1024
SWIGLU_LIMIT = 7.0
SWIGLU_ALPHA = 1.702
NORM_EPS = 1e-6
NEG_INF = -1e30
Q_SCALE = float(np.log2(np.e) / np.sqrt(HEAD_DIM))

LANES = 128
CHUNK = 16

TM = 512
RB = 512
FC = 512
PAIRS = TM * TOP_K
SORT_ROWS = 2560
CHUNKS_PER_BLOCK = RB // CHUNK
CHUNKS_PER_TILE = SORT_ROWS // CHUNK
DUMP_ROWS = 2 * RB
VMEM_LIMIT = 48 * 1024 * 1024
EXPERT_VMEM_LIMIT = 56 * 1024 * 1024

assert SORT_ROWS % TM == 0 and SORT_ROWS >= PAIRS + N_EXPERTS * (CHUNK - 1)


def _rms(x, g):
    return x * lax.rsqrt(jnp.mean(x * x, axis=-1, keepdims=True) + NORM_EPS) * g


def _nt(a, b):
    return lax.dot_general(a, b, (((1,), (1,)), ((), ())), preferred_element_type=F32)


def _proj_kernel(x_ref, mixn_ref, wk_ref, wqvt_ref, wuv_ref, cos_ref, sin_ref, cost_ref, sint_ref, lng_ref,
                 lnb_ref, ws_ref, bs_ref, gon_ref, qt_ref, k_ref, vt_ref, kmean_ref, gm_ref, gacc_ref):
    h = _rms(x_ref[...], mixn_ref[...]).astype(BF16)
    half = HEAD_DIM // 2

    uv = jnp.dot(h, wuv_ref[...], preferred_element_type=F32)
    kx = jnp.dot(h, wk_ref[...], preferred_element_type=F32)
    qvt = _nt(wqvt_ref[...], h)
    inv_sqrt2 = np.float32(1.0 / np.sqrt(2.0))

    def gelu(t):
        return 0.5 * t * (1.0 + lax.erf(t * inv_sqrt2))

    gu = gelu(uv[:, :GMLP_WIDTH])
    gv = gelu(uv[:, GMLP_WIDTH:])
    row = lax.broadcasted_iota(I32, (GMLP_CHUNK, GMLP_CHUNK), 0)
    col = lax.broadcasted_iota(I32, (GMLP_CHUNK, GMLP_CHUNK), 1)
    tril = col <= row
    for g in range(GMLP_GROUPS):
        sl = slice(g * GMLP_GROUP_DIM, (g + 1) * GMLP_GROUP_DIM)
        vg = gv[:, sl]
        mu = jnp.mean(vg, axis=-1, keepdims=True)
        dv = vg - mu
        var = jnp.mean(dv * dv, axis=-1, keepdims=True)
        vn = (dv * lax.rsqrt(var + NORM_EPS) * lng_ref[:, sl] + lnb_ref[:, sl]).astype(BF16)
        w = jnp.where(tril, ws_ref[g], 0.0).astype(BF16)
        bias = bs_ref[g]
        for c in range(TM // GMLP_CHUNK):
            rs = slice(c * GMLP_CHUNK, (c + 1) * GMLP_CHUNK)
            mixed = jnp.dot(w, vn[rs], preferred_element_type=F32) + bias
            gacc_ref[rs, sl] = gu[rs, sl] * mixed
    gm_ref[...] = _rms(gacc_ref[...], gon_ref[...]).astype(BF16)

    lane = lax.broadcasted_iota(I32, (TM, LANES), 1)
    first_half = (lane % HEAD_DIM) < half
    parts = []
    for s in range(ATTN_WIDTH // LANES):
        slab = kx[:, s * LANES:(s + 1) * LANES]
        ahead = pltpu.roll(slab, LANES - half, 1)
        behind = pltpu.roll(slab, half, 1)
        parts.append(jnp.where(first_half, ahead, behind))
    k = kx * cos_ref[...] + jnp.concatenate(parts, axis=1) * sin_ref[...]
    k_ref[...] = k.astype(BF16)
    for c in range(TM // MOBA_BLOCK):
        kmean_ref[0, c:c + 1, :] = jnp.mean(k[c * MOBA_BLOCK:(c + 1) * MOBA_BLOCK], axis=0, keepdims=True)

    qx = qvt[:ATTN_WIDTH]
    swapped = []
    for hd in range(ATTN_WIDTH // HEAD_DIM):
        r0 = hd * HEAD_DIM
        swapped += [qx[r0 + half:r0 + HEAD_DIM], qx[r0:r0 + half]]
    qt = (qx * cost_ref[...] + jnp.concatenate(swapped, axis=0) * sint_ref[...]) * Q_SCALE
    qt_ref[...] = qt.astype(BF16)
    vt_ref[...] = qvt[ATTN_WIDTH:].astype(BF16)


def _attn_kernel(qt_ref, k_ref, vt_ref, kmean_ref, o_ref):
    seq = k_ref.shape[0]
    n_blk = seq // MOBA_BLOCK
    feat = lax.broadcasted_iota(I32, (LANES, 1), 0)
    lane = lax.broadcasted_iota(I32, (1, LANES), 1)
    blk_of_key = lax.broadcasted_iota(I32, (n_blk, seq), 0)
    blk_of_query = lax.broadcasted_iota(I32, (n_blk, seq), 1) // MOBA_BLOCK
    key_pos = lax.broadcasted_iota(I32, (MOBA_BLOCK, MOBA_BLOCK), 0)
    query_pos = lax.broadcasted_iota(I32, (MOBA_BLOCK, MOBA_BLOCK), 1)
    causal = key_pos <= query_pos
    blk = lambda i: slice(i * MOBA_BLOCK, (i + 1) * MOBA_BLOCK)
    ones = jnp.ones((16, seq), BF16)

    n_heads = LANES // HEAD_DIM
    dotf = lambda a, b: jnp.dot(a, b, preferred_element_type=F32)
    qts, biases = [], []
    for h in range(n_heads):
        rows_in_head = (feat >= h * HEAD_DIM) & (feat < (h + 1) * HEAD_DIM)
        qt = jnp.where(rows_in_head, qt_ref[...], jnp.zeros((), BF16))

        km = jnp.where((lane >= h * HEAD_DIM) & (lane < (h + 1) * HEAD_DIM), kmean_ref[0], 0.0)
        km1 = km.astype(BF16)
        r1 = km - km1.astype(F32)
        km2 = r1.astype(BF16)
        km3 = (r1 - km2.astype(F32)).astype(BF16)
        gate = dotf(km1, qt) + dotf(km2, qt) + dotf(km3, qt)
        rank = jnp.zeros((n_blk, seq), F32)
        for m in range(n_blk):
            gm_ = gate[m:m + 1, :]
            beats = (m < blk_of_query) & ((gm_ > gate) | ((gm_ == gate) & (m < blk_of_key)))
            rank = rank + jnp.where(beats, 1.0, 0.0)
        chosen = (blk_of_key < blk_of_query) & (rank < MOBA_TOPK)
        qts.append(qt)
        biases.append(jnp.where(chosen, 0.0, NEG_INF))

    def scores(i):
        return dotf(k_ref[0:(i + 1) * MOBA_BLOCK, :], jnp.concatenate([qt[:, blk(i)] for qt in qts], axis=1))

    def finish(i, h, pu):
        hs = slice(h * HEAD_DIM, (h + 1) * HEAD_DIM)
        n_keys = (i + 1) * MOBA_BLOCK
        ov = dotf(jnp.concatenate([vt_ref[hs, 0:n_keys], ones[:, 0:n_keys]], axis=0), pu)
        o_ref[i, hs, :] = ov[0:HEAD_DIM] * (1.0 / ov[HEAD_DIM:HEAD_DIM + 1])

    order = list(range(n_blk - 1, -1, -1))
    s_next = scores(order[0])
    unfinished = []
    for pos, i in enumerate(order):
        s_pair = s_next
        if pos + 1 < n_blk:
            s_next = scores(order[pos + 1])
        probs = []
        for h in range(n_heads):
            s = s_pair[:, blk(h)]
            bias = biases[h]
            tiles = [s[blk(j)] + bias[j:j + 1, blk(i)] for j in range(i)]
            tiles.append(jnp.where(causal, s[blk(i)], NEG_INF))
            top = tiles[0]
            for t in tiles[1:]:
                top = jnp.maximum(top, t)
            m_q = jnp.max(top, axis=0, keepdims=True)
            probs.append(jnp.concatenate([jnp.exp2(t - m_q).astype(BF16) for t in tiles], axis=0))
        for args in unfinished:
            finish(*args)
        unfinished = [(i, h, probs[h]) for h in range(n_heads)]
    for args in unfinished:
        finish(*args)


def _mix_kernel(x_ref, at_ref, gm_ref, aon_ref, woa_ref, wog_ref, ffn_ref, rwt_ref, rb_ref,
                x1_ref, xs_ref, lpos_ref, gate_ref, tab_ref, tot_ref, carry_ref):
    @pl.when(pl.program_id(0) == 0)
    def _():
        carry_ref[...] = jnp.zeros_like(carry_ref)

    parts = []
    for c in range(TM // MOBA_BLOCK):
        at = at_ref[c]
        ss = jnp.sum(at * at, axis=0, keepdims=True)
        atn = at * lax.rsqrt(ss * (1.0 / ATTN_WIDTH) + NORM_EPS) * aon_ref[...]
        parts.append(atn.T)
    attn = jnp.concatenate(parts, axis=0).astype(BF16)
    x1 = (x_ref[...] + jnp.dot(attn, woa_ref[...], preferred_element_type=F32)
          + jnp.dot(gm_ref[...], wog_ref[...], preferred_element_type=F32))
    x1_ref[...] = x1
    hn = _rms(x1, ffn_ref[...])

    hn_b = hn.astype(BF16)
    hn_lo = (hn - hn_b.astype(F32)).astype(BF16)
    rw = rwt_ref[...]
    rw_hi = rw.astype(BF16)
    rw_lo = (rw - rw_hi.astype(F32)).astype(BF16)
    both = _nt(jnp.concatenate([rw_hi, rw_lo], axis=0), hn_b)
    logits = both[:N_EXPERTS] + both[N_EXPERTS:] + _nt(rw_hi, hn_lo) + rb_ref[...]
    eidx = lax.broadcasted_iota(I32, (N_EXPERTS, TM), 0)
    vals, idxs = [], []
    rest = logits
    for _ in range(TOP_K):
        m = jnp.max(rest, axis=0, keepdims=True)
        idx = jnp.min(jnp.where(rest == m, eidx, N_EXPERTS), axis=0, keepdims=True)
        vals.append(m)
        idxs.append(idx)
        rest = jnp.where(eidx == idx, -jnp.inf, rest)
    exps = [jnp.exp(v - vals[0]) for v in vals]
    denom = exps[0] + exps[1] + exps[2] + exps[3]
    hot = [eidx == idx for idx in idxs]
    multi = jnp.where(hot[0] | hot[1] | hot[2] | hot[3], 1.0, 0.0)

    tp = lax.broadcasted_iota(I32, (TM, TM), 0)
    tq = lax.broadcasted_iota(I32, (TM, TM), 1)
    earlier = jnp.where(tp < tq, 1.0, 0.0).astype(BF16)
    before = jnp.dot(multi.astype(BF16), earlier, preferred_element_type=F32)
    count = jnp.sum(multi, axis=1, keepdims=True)
    seg = jnp.floor((count + (CHUNK - 1)) * (1.0 / CHUNK))
    ep = lax.broadcasted_iota(I32, (N_EXPERTS, N_EXPERTS), 0)
    eq = lax.broadcasted_iota(I32, (N_EXPERTS, N_EXPERTS), 1)
    lower = jnp.where(eq < ep, 1.0, 0.0).astype(BF16)
    seg_wide = jnp.broadcast_to(seg, (N_EXPERTS, LANES))
    off = jnp.dot(lower, seg_wide.astype(BF16), preferred_element_type=F32)
    slot = off[:, 0:1] * CHUNK + before
    lpos = [jnp.sum(jnp.where(hot[kk], slot, 0.0), axis=0, keepdims=True).astype(I32) for kk in range(TOP_K)]
    for kk in range(TOP_K):
        lpos_ref[kk:kk + 1, :] = lpos[kk]
        gate_ref[kk:kk + 1, :] = exps[kk] / denom
    def onehot_rows(c):
        r = lax.broadcasted_iota(I32, (TM, TM), 0) + c * TM
        onehot = jnp.zeros((TM, TM), F32)
        for kk in range(TOP_K):
            onehot = jnp.where(r == lpos[kk], 1.0, onehot)
        return onehot.astype(BF16)

    n_chunks = SORT_ROWS // TM
    nxt = onehot_rows(0)
    for c in range(n_chunks):
        cur = nxt
        if c + 1 < n_chunks:
            nxt = onehot_rows(c + 1)
        xs_ref[c * TM:(c + 1) * TM, :] = jnp.dot(cur, hn_b, preferred_element_type=F32).astype(BF16)
    lane = lax.broadcasted_iota(I32, (N_EXPERTS, LANES), 1)
    tab = jnp.where(lane == 0, seg_wide, jnp.where(lane == 1, off, carry_ref[...]))
    tab_ref[0] = tab.astype(I32)
    carry_ref[...] = carry_ref[...] + seg
    tot_ref[...] = carry_ref[...].astype(I32)


def _chunk_rows(ref, chunk_index):
    return ref.at[pl.ds(pl.multiple_of(chunk_index * CHUNK, CHUNK), CHUNK), :]


def _expert_kernel(blke_ref, nused_ref, tab_prev_ref, tab_ref, tab_next_ref, xs_ref, wgu_ref, bgu_ref, wdn_ref,
                   bdn_ref, y_ref,
                   xbuf, ybuf, zbuf, wgu_bf, wdn_bf, sem_in, sem_out, sem_fill):
    i = pl.program_id(0)
    n_used = nused_ref[0]
    used = i < n_used
    slot = i % 2

    def copies_in(tab, s):
        return [pltpu.make_async_copy(_chunk_rows(xs_ref, tab[0, 0, c]), _chunk_rows(xbuf.at[s], c), sem_in.at[s])
                for c in range(CHUNKS_PER_BLOCK)]

    def copies_out(tab, s):
        return [pltpu.make_async_copy(_chunk_rows(ybuf.at[s], c), _chunk_rows(y_ref, tab[0, 0, CHUNKS_PER_BLOCK + c]),
                                      sem_out.at[s]) for c in range(CHUNKS_PER_BLOCK)]

    @pl.when(i == 0)
    def _():
        zbuf[...] = jnp.zeros_like(zbuf)
        n_regions = y_ref.shape[0] // SORT_ROWS
        tail = SORT_ROWS - PAIRS
        starts = [t * SORT_ROWS + PAIRS for t in range(n_regions)]
        starts += [n_regions * SORT_ROWS + d * tail for d in range(DUMP_ROWS // tail)]
        fills = [pltpu.make_async_copy(zbuf, y_ref.at[pl.ds(r0, tail), :], sem_fill) for r0 in starts]
        for f in fills:
            f.start()
        for f in fills:
            f.wait()
        for cp in copies_in(tab_ref, 0):
            cp.start()
        ybuf[1] = jnp.zeros((RB, D_MODEL), BF16)

    @pl.when(used & ((i == 0) | (blke_ref[i] != blke_ref[jnp.maximum(i - 1, 0)])))
    def _():
        for c in range(2 * D_EXPERT // FC):
            wgu_bf[:, c * FC:(c + 1) * FC] = wgu_ref[0, :, c * FC:(c + 1) * FC].astype(BF16)
        for c in range(D_EXPERT // FC):
            wdn_bf[c * FC:(c + 1) * FC, :] = wdn_ref[0, c * FC:(c + 1) * FC, :].astype(BF16)

    @pl.when(used)
    def _():
        for cp in copies_in(tab_ref, slot):
            cp.wait()

        @pl.when(i >= 1)
        def _():
            for cp in copies_out(tab_ref, slot):
                cp.wait()

        x = xbuf[slot]
        y = jnp.zeros((RB, D_MODEL), F32) + bdn_ref[0]
        for c in range(D_EXPERT // FC):
            g = jnp.dot(x, wgu_bf[:, c * FC:(c + 1) * FC], preferred_element_type=F32) \
                + bgu_ref[0, :, c * FC:(c + 1) * FC]
            lin = jnp.dot(x, wgu_bf[:, D_EXPERT + c * FC:D_EXPERT + (c + 1) * FC],
                          preferred_element_type=F32) + bgu_ref[0, :, D_EXPERT + c * FC:D_EXPERT + (c + 1) * FC]
            if c == 0:
                for cp in copies_in(tab_next_ref, 1 - slot):
                    cp.start()
            g = jnp.minimum(g, SWIGLU_LIMIT)
            lin = jnp.clip(lin, -SWIGLU_LIMIT, SWIGLU_LIMIT)
            a = g * (1.0 / (1.0 + jnp.exp(-SWIGLU_ALPHA * g))) * (lin + 1.0)
            y = y + jnp.dot(a.astype(BF16), wdn_bf[c * FC:(c + 1) * FC, :], preferred_element_type=F32)
            if c == 0:
                for cp in copies_out(tab_prev_ref, 1 - slot):
                    cp.start()
        ybuf[slot] = y.astype(BF16)

        @pl.when(i == n_used - 1)
        def _():
            for cp in copies_out(tab_prev_ref, 1 - slot):
                cp.wait()
            last = copies_out(tab_ref, slot)
            for cp in last:
                cp.start()
            for cp in copies_in(tab_next_ref, 1 - slot) + last:
                cp.wait()


def _final_kernel(y_ref, x1_ref, lpos_ref, gate_ref, p_ref, plen_ref, wg_ref, wp_ref, post_ref, fin_ref, o_ref):
    lpos = lpos_ref[...]
    gates = gate_ref[...]
    def gate_cols(c):
        r = lax.broadcasted_iota(I32, (TM, TM), 1) + c * TM
        w = jnp.zeros((TM, TM), F32)
        for kk in range(TOP_K):
            w = jnp.where(r == lpos[:, kk:kk + 1], gates[:, kk:kk + 1], w)
        return w.astype(BF16)

    emb = jnp.dot(p_ref[...].astype(BF16), wp_ref[...], preferred_element_type=F32)
    n_chunks = SORT_ROWS // TM
    moe = jnp.zeros((TM, D_MODEL), F32)
    nxt = gate_cols(0)
    for c in range(n_chunks):
        cur = nxt
        if c + 1 < n_chunks:
            nxt = gate_cols(c + 1)
        moe = moe + jnp.dot(cur, y_ref[c * TM:(c + 1) * TM, :], preferred_element_type=F32)
    x2 = x1_ref[...] + moe
    z = jnp.dot(_rms(x2, plen_ref[...]).astype(BF16), wg_ref[...], preferred_element_type=F32)
    ple_gate = 1.0 / (1.0 + jnp.exp(-z))
    x3 = x2 + ple_gate * _rms(emb, post_ref[...])
    o_ref[...] = _rms(x3, fin_ref[...])


def _params(sem):
    return pltpu.CompilerParams(dimension_semantics=sem, vmem_limit_bytes=VMEM_LIMIT)


def _rope_tables(seq):
    half = HEAD_DIM // 2
    inv = ROPE_THETA ** (-jnp.arange(half, dtype=F32) / half)
    ang = jnp.arange(seq, dtype=F32)[:, None] * inv[None, :]
    cos = jnp.cos(ang)
    sin = jnp.sin(ang)
    n_heads = ATTN_WIDTH // HEAD_DIM
    cos_t = jnp.tile(jnp.concatenate([cos, cos], axis=1), (1, n_heads))
    sin_t = jnp.tile(jnp.concatenate([-sin, sin], axis=1), (1, n_heads))
    return cos_t, sin_t


def _layer(x, p, mix_norm, w_in, gmlp_ln_g, gmlp_ln_b, gmlp_ws, gmlp_bs, attn_out_norm, gmlp_out_norm,
           w_out, ffn_norm, router_w, router_b, w_gate_up, b_gate_up, w_down, b_down, ple_norm,
           w_ple_gate, w_ple_proj, ple_post_norm, out_norm):
    B, S, D = x.shape
    T = B * S
    A = ATTN_WIDTH
    n_tiles = T // TM
    n_blk = S // MOBA_BLOCK
    tiles_per_seq = S // TM
    xf = x.reshape(T, D)
    row2 = lambda v: v.reshape(1, -1)
    full = lambda shape: pl.BlockSpec(shape, lambda *_: (0,) * len(shape))

    cos_t, sin_t = _rope_tables(S)
    wk = w_in[:, A:2 * A].astype(BF16)
    wqvt = jnp.concatenate([w_in[:, :A], w_in[:, 2 * A:3 * A]], axis=1).T.astype(BF16)
    wuv = w_in[:, 3 * A:].astype(BF16)

    qt, k, vt, kmean, gm = pl.pallas_call(
        _proj_kernel,
        grid=(n_tiles,),
        in_specs=[pl.BlockSpec((TM, D), lambda i: (i, 0)), full((1, D)), full((D, A)), full((2 * A, D)),
                  full((D, 2 * GMLP_WIDTH)),
                  pl.BlockSpec((TM, A), lambda i: (i % tiles_per_seq, 0)),
                  pl.BlockSpec((TM, A), lambda i: (i % tiles_per_seq, 0)),
                  pl.BlockSpec((A, TM), lambda i: (0, i % tiles_per_seq)),
                  pl.BlockSpec((A, TM), lambda i: (0, i % tiles_per_seq)),
                  full((1, GMLP_WIDTH)), full((1, GMLP_WIDTH)),
                  full((GMLP_GROUPS, GMLP_CHUNK, GMLP_CHUNK)), full((GMLP_GROUPS, GMLP_CHUNK, 1)),
                  full((1, GMLP_WIDTH))],
        out_specs=[pl.BlockSpec((A, TM), lambda i: (0, i)), pl.BlockSpec((TM, A), lambda i: (i, 0)),
                   pl.BlockSpec((A, TM), lambda i: (0, i)),
                   pl.BlockSpec((1, TM // MOBA_BLOCK, A), lambda i: (i, 0, 0)),
                   pl.BlockSpec((TM, GMLP_WIDTH), lambda i: (i, 0))],
        out_shape=[jax.ShapeDtypeStruct((A, T), BF16), jax.ShapeDtypeStruct((T, A), BF16),
                   jax.ShapeDtypeStruct((A, T), BF16),
                   jax.ShapeDtypeStruct((n_tiles, TM // MOBA_BLOCK, A), F32),
                   jax.ShapeDtypeStruct((T, GMLP_WIDTH), BF16)],
        scratch_shapes=[pltpu.VMEM((TM, GMLP_WIDTH), F32)],
        compiler_params=_params(("parallel",)),
        name="proj",
    )(xf, row2(mix_norm), wk, wqvt, wuv, cos_t, sin_t, cos_t.T, sin_t.T, row2(gmlp_ln_g), row2(gmlp_ln_b), gmlp_ws,
      gmlp_bs.reshape(GMLP_GROUPS, GMLP_CHUNK, 1), row2(gmlp_out_norm))

    kmean = kmean.reshape(B, n_blk, A)
    n_pairs = A // LANES
    attn_t = pl.pallas_call(
        _attn_kernel,
        grid=(B, n_pairs),
        in_specs=[pl.BlockSpec((LANES, S), lambda b, h: (h, b)), pl.BlockSpec((S, LANES), lambda b, h: (b, h)),
                  pl.BlockSpec((LANES, S), lambda b, h: (h, b)),
                  pl.BlockSpec((1, n_blk, LANES), lambda b, h: (b, 0, h))],
        out_specs=pl.BlockSpec((n_blk, LANES, MOBA_BLOCK), lambda b, h: (b, h, 0)),
        out_shape=jax.ShapeDtypeStruct((T // MOBA_BLOCK, A, MOBA_BLOCK), F32),
        compiler_params=_params(("parallel", "parallel")),
        name="attn",
    )(qt, k, vt, kmean)

    x1, xs, lpos, gates, tab, tot = pl.pallas_call(
        _mix_kernel,
        grid=(n_tiles,),
        in_specs=[pl.BlockSpec((TM, D), lambda i: (i, 0)),
                  pl.BlockSpec((TM // MOBA_BLOCK, A, MOBA_BLOCK), lambda i: (i, 0, 0)),
                  pl.BlockSpec((TM, GMLP_WIDTH), lambda i: (i, 0)),
                  full((A, 1)), full((A, D)), full((GMLP_WIDTH, D)), full((1, D)),
                  full((N_EXPERTS, D)), full((N_EXPERTS, 1))],
        out_specs=[pl.BlockSpec((TM, D), lambda i: (i, 0)), pl.BlockSpec((SORT_ROWS, D), lambda i: (i, 0)),
                   pl.BlockSpec((TOP_K, TM), lambda i: (0, i)), pl.BlockSpec((TOP_K, TM), lambda i: (0, i)),
                   pl.BlockSpec((1, N_EXPERTS, LANES), lambda i: (i, 0, 0)), full((N_EXPERTS, LANES))],
        out_shape=[jax.ShapeDtypeStruct((T, D), F32), jax.ShapeDtypeStruct((n_tiles * SORT_ROWS, D), BF16),
                   jax.ShapeDtypeStruct((TOP_K, T), I32), jax.ShapeDtypeStruct((TOP_K, T), F32),
                   jax.ShapeDtypeStruct((n_tiles, N_EXPERTS, LANES), I32),
                   jax.ShapeDtypeStruct((N_EXPERTS, LANES), I32)],
        scratch_shapes=[pltpu.VMEM((N_EXPERTS, LANES), F32)],
        compiler_params=_params(("arbitrary",)),
        name="mix",
    )(xf, attn_t, gm, attn_out_norm.reshape(A, 1), w_out[:A].astype(BF16), w_out[A:].astype(BF16),
      row2(ffn_norm), router_w.T, router_b.reshape(N_EXPERTS, 1))

    n_blocks = -(-(T * TOP_K + n_tiles * N_EXPERTS * (CHUNK - 1) + N_EXPERTS * (RB - CHUNK)) // RB)
    seg16, off16, before16 = tab[:, :, 0], tab[:, :, 1], tab[:, :, 2]
    tot16 = tot[:, 0]
    blocks_e = (tot16 + CHUNKS_PER_BLOCK - 1) // CHUNKS_PER_BLOCK
    bend = jnp.cumsum(blocks_e)
    nused = bend[-1:].astype(I32)
    bidx = jnp.arange(n_blocks, dtype=I32)
    blk_e = jnp.minimum(jnp.sum(bend[None, :] <= bidx[:, None], axis=1), N_EXPERTS - 1).astype(I32)
    sel = blk_e[:, None] == jnp.arange(N_EXPERTS, dtype=I32)[None, :]
    pick = lambda v: jnp.sum(jnp.where(sel, v[None, :], 0), axis=1)
    chunk_pos = ((bidx - pick(bend - blocks_e)) * CHUNKS_PER_BLOCK)[:, None] \
        + jnp.arange(CHUNKS_PER_BLOCK, dtype=I32)[None, :]
    valid = (chunk_pos < pick(tot16)[:, None]) & (bidx < nused)[:, None]
    seg_end = jnp.sum(jnp.where(sel[:, None, :], (before16 + seg16)[None], 0), axis=2)
    shift = jnp.sum(jnp.where(sel[:, None, :], (off16 - before16)[None], 0), axis=2)
    tile = jnp.minimum(jnp.sum(seg_end[:, None, :] <= chunk_pos[:, :, None], axis=2), n_tiles - 1)
    in_tile = tile[:, :, None] == jnp.arange(n_tiles, dtype=I32)[None, None, :]
    where16 = tile * CHUNKS_PER_TILE + jnp.sum(jnp.where(in_tile, shift[:, None, :], 0), axis=2) + chunk_pos
    dump16 = (n_tiles * CHUNKS_PER_TILE + (bidx % 2)[:, None] * CHUNKS_PER_BLOCK
              + jnp.arange(CHUNKS_PER_BLOCK, dtype=I32)[None, :])
    tabs = jnp.concatenate([jnp.where(valid, where16, 0), jnp.where(valid, where16, dump16)],
                           axis=1).astype(I32).reshape(n_blocks, 1, 2 * CHUNKS_PER_BLOCK)

    last = lambda i, nu: jnp.minimum(i, nu[0] - 1)
    smem_tab = lambda step: pl.BlockSpec((1, 1, 2 * CHUNKS_PER_BLOCK),
                                         lambda i, be, nu: (jnp.clip(i + step, 0, n_blocks - 1), 0, 0),
                                         memory_space=pltpu.SMEM)
    y = pl.pallas_call(
        _expert_kernel,
        grid_spec=pltpu.PrefetchScalarGridSpec(
            num_scalar_prefetch=2, grid=(n_blocks,),
            in_specs=[smem_tab(-1), smem_tab(0), smem_tab(1), pl.BlockSpec(memory_space=pl.ANY),
                      pl.BlockSpec((1, D, 2 * D_EXPERT), lambda i, be, nu: (be[last(i, nu)], 0, 0)),
                      pl.BlockSpec((1, 1, 2 * D_EXPERT), lambda i, be, nu: (be[last(i, nu)], 0, 0)),
                      pl.BlockSpec((1, D_EXPERT, D), lambda i, be, nu: (be[last(i, nu)], 0, 0)),
                      pl.BlockSpec((1, 1, D), lambda i, be, nu: (be[last(i, nu)], 0, 0))],
            out_specs=pl.BlockSpec(memory_space=pl.ANY),
            scratch_shapes=[pltpu.VMEM((2, RB, D), BF16), pltpu.VMEM((2, RB, D), BF16),
                            pltpu.VMEM((SORT_ROWS - PAIRS, D), BF16),
                            pltpu.VMEM((D, 2 * D_EXPERT), BF16), pltpu.VMEM((D_EXPERT, D), BF16),
                            pltpu.SemaphoreType.DMA((2,)), pltpu.SemaphoreType.DMA((2,)),
                            pltpu.SemaphoreType.DMA(())]),
        out_shape=jax.ShapeDtypeStruct((n_tiles * SORT_ROWS + DUMP_ROWS, D), BF16),
        compiler_params=pltpu.CompilerParams(dimension_semantics=("arbitrary",),
                                             vmem_limit_bytes=EXPERT_VMEM_LIMIT),
        name="experts",
    )(blk_e, nused, tabs, tabs, tabs, xs, w_gate_up, b_gate_up.reshape(N_EXPERTS, 1, -1), w_down,
      b_down.reshape(N_EXPERTS, 1, -1))

    out = pl.pallas_call(
        _final_kernel,
        grid=(n_tiles,),
        in_specs=[pl.BlockSpec((SORT_ROWS, D), lambda i: (i, 0)),
                  pl.BlockSpec((TM, D), lambda i: (i, 0)), pl.BlockSpec((TM, TOP_K), lambda i: (i, 0)),
                  pl.BlockSpec((TM, TOP_K), lambda i: (i, 0)),
                  pl.BlockSpec((TM, PLE_DIM), lambda i: (i, 0)), full((1, D)), full((D, D)), full((PLE_DIM, D)),
                  full((1, D)), full((1, D))],
        out_specs=pl.BlockSpec((TM, D), lambda i: (i, 0)),
        out_shape=jax.ShapeDtypeStruct((T, D), F32),
        compiler_params=_params(("parallel",)),
        name="final",
    )(y, x1, lpos.T, gates.T, p.reshape(T, PLE_DIM), row2(ple_norm), w_ple_gate.astype(BF16),
      w_ple_proj.astype(BF16), row2(ple_post_norm), row2(out_norm))
    return out.reshape(B, S, D)


def kernel(x, p, mix_norm, w_in, gmlp_ln_g, gmlp_ln_b, gmlp_ws, gmlp_bs, attn_out_norm, gmlp_out_norm, w_out, ffn_norm, router_w, router_b, w_gate_up, b_gate_up, w_down, b_down, ple_norm, w_ple_gate, w_ple_proj, ple_post_norm, final_norm):
    depth = p.shape[0]
    assert depth == 1, "the final rmsnorm is fused into the (single) layer"
    return _layer(x, p[0], mix_norm[0], w_in[0], gmlp_ln_g[0], gmlp_ln_b[0], gmlp_ws[0], gmlp_bs[0],
                  attn_out_norm[0], gmlp_out_norm[0], w_out[0], ffn_norm[0], router_w[0], router_b[0],
                  w_gate_up[0], b_gate_up[0], w_down[0], b_down[0], ple_norm[0], w_ple_gate[0],
                  w_ple_proj[0], ple_post_norm[0], final_norm)
```

```python
import jax
import jax.numpy as jnp
import numpy as np
from jax import lax
from jax.experimental import pallas as pl
from jax.experimental.pallas import tpu as pltpu

F32 = jnp.float32
BF16 = jnp.bfloat16
I32 = jnp.int32

D_MODEL = 1024
PLE_DIM = 256
ATTN_WIDTH = 512
HEAD_DIM = 64
ROPE_THETA = 10000.0
MOBA_BLOCK = 256
MOBA_TOPK = 3
GMLP_WIDTH = 512
GMLP_GROUPS = 4
GMLP_GROUP_DIM = 128
GMLP_CHUNK = 128
N_EXPERTS = 32
TOP_K = 4
D_EXPERT = 1024
SWIGLU_LIMIT = 7.0
SWIGLU_ALPHA = 1.702
NORM_EPS = 1e-6
NEG_INF = -1e30
Q_SCALE = float(np.log2(np.e) / np.sqrt(HEAD_DIM))

LANES = 128
CHUNK = 16

TM = 512
RB = 512
FC = 512
PAIRS = TM * TOP_K
SORT_ROWS = 2560
CHUNKS_PER_BLOCK = RB // CHUNK
CHUNKS_PER_TILE = SORT_ROWS // CHUNK
DUMP_ROWS = 2 * RB
VMEM_LIMIT = 48 * 1024 * 1024
EXPERT_VMEM_LIMIT = 56 * 1024 * 1024

assert SORT_ROWS % TM == 0 and SORT_ROWS >= PAIRS + N_EXPERTS * (CHUNK - 1)


def _rms(x, g):
    return x * lax.rsqrt(jnp.mean(x * x, axis=-1, keepdims=True) + NORM_EPS) * g


def _nt(a, b):
    return lax.dot_general(a, b, (((1,), (1,)), ((), ())), preferred_element_type=F32)


def _proj_kernel(x_ref, mixn_ref, wk_ref, wqvt_ref, wuv_ref, cos_ref, sin_ref, cost_ref, sint_ref, lng_ref,
                 lnb_ref, ws_ref, bs_ref, gon_ref, qt_ref, k_ref, vt_ref, kmean_ref, gm_ref, gacc_ref):
    h = _rms(x_ref[...], mixn_ref[...]).astype(BF16)
    half = HEAD_DIM // 2

    uv = jnp.dot(h, wuv_ref[...], preferred_element_type=F32)
    kx = jnp.dot(h, wk_ref[...], preferred_element_type=F32)
    qvt = _nt(wqvt_ref[...], h)
    inv_sqrt2 = np.float32(1.0 / np.sqrt(2.0))

    def gelu(t):
        return 0.5 * t * (1.0 + lax.erf(t * inv_sqrt2))

    gu = gelu(uv[:, :GMLP_WIDTH])
    gv = gelu(uv[:, GMLP_WIDTH:])
    row = lax.broadcasted_iota(I32, (GMLP_CHUNK, GMLP_CHUNK), 0)
    col = lax.broadcasted_iota(I32, (GMLP_CHUNK, GMLP_CHUNK), 1)
    tril = col <= row
    for g in range(GMLP_GROUPS):
        sl = slice(g * GMLP_GROUP_DIM, (g + 1) * GMLP_GROUP_DIM)
        vg = gv[:, sl]
        mu = jnp.mean(vg, axis=-1, keepdims=True)
        dv = vg - mu
        var = jnp.mean(dv * dv, axis=-1, keepdims=True)
        vn = (dv * lax.rsqrt(var + NORM_EPS) * lng_ref[:, sl] + lnb_ref[:, sl]).astype(BF16)
        w = jnp.where(tril, ws_ref[g], 0.0).astype(BF16)
        bias = bs_ref[g]
        for c in range(TM // GMLP_CHUNK):
            rs = slice(c * GMLP_CHUNK, (c + 1) * GMLP_CHUNK)
            mixed = jnp.dot(w, vn[rs], preferred_element_type=F32) + bias
            gacc_ref[rs, sl] = gu[rs, sl] * mixed
    gm_ref[...] = _rms(gacc_ref[...], gon_ref[...]).astype(BF16)

    lane = lax.broadcasted_iota(I32, (TM, LANES), 1)
    first_half = (lane % HEAD_DIM) < half
    parts = []
    for s in range(ATTN_WIDTH // LANES):
        slab = kx[:, s * LANES:(s + 1) * LANES]
        ahead = pltpu.roll(slab, LANES - half, 1)
        behind = pltpu.roll(slab, half, 1)
        parts.append(jnp.where(first_half, ahead, behind))
    k = kx * cos_ref[...] + jnp.concatenate(parts, axis=1) * sin_ref[...]
    k_ref[...] = k.astype(BF16)
    for c in range(TM // MOBA_BLOCK):
        kmean_ref[0, c:c + 1, :] = jnp.mean(k[c * MOBA_BLOCK:(c + 1) * MOBA_BLOCK], axis=0, keepdims=True)

    qx = qvt[:ATTN_WIDTH]
    swapped = []
    for hd in range(ATTN_WIDTH // HEAD_DIM):
        r0 = hd * HEAD_DIM
        swapped += [qx[r0 + half:r0 + HEAD_DIM], qx[r0:r0 + half]]
    qt = (qx * cost_ref[...] + jnp.concatenate(swapped, axis=0) * sint_ref[...]) * Q_SCALE
    qt_ref[...] = qt.astype(BF16)
    vt_ref[...] = qvt[ATTN_WIDTH:].astype(BF16)


def _attn_kernel(qt_ref, k_ref, vt_ref, kmean_ref, o_ref):
    seq = k_ref.shape[0]
    n_blk = seq // MOBA_BLOCK
    feat = lax.broadcasted_iota(I32, (LANES, 1), 0)
    lane = lax.broadcasted_iota(I32, (1, LANES), 1)
    blk_of_key = lax.broadcasted_iota(I32, (n_blk, seq), 0)
    blk_of_query = lax.broadcasted_iota(I32, (n_blk, seq), 1) // MOBA_BLOCK
    key_pos = lax.broadcasted_iota(I32, (MOBA_BLOCK, MOBA_BLOCK), 0)
    query_pos = lax.broadcasted_iota(I32, (MOBA_BLOCK, MOBA_BLOCK), 1)
    causal = key_pos <= query_pos
    blk = lambda i: slice(i * MOBA_BLOCK, (i + 1) * MOBA_BLOCK)
    ones = jnp.ones((16, seq), BF16)

    n_heads = LANES // HEAD_DIM
    dotf = lambda a, b: jnp.dot(a, b, preferred_element_type=F32)
    qts, biases = [], []
    for h in range(n_heads):
        rows_in_head = (feat >= h * HEAD_DIM) & (feat < (h + 1) * HEAD_DIM)
        qt = jnp.where(rows_in_head, qt_ref[...], jnp.zeros((), BF16))

        km = jnp.where((lane >= h * HEAD_DIM) & (lane < (h + 1) * HEAD_DIM), kmean_ref[0], 0.0)
        km1 = km.astype(BF16)
        r1 = km - km1.astype(F32)
        km2 = r1.astype(BF16)
        km3 = (r1 - km2.astype(F32)).astype(BF16)
        gate = dotf(km1, qt) + dotf(km2, qt) + dotf(km3, qt)
        rank = jnp.zeros((n_blk, seq), F32)
        for m in range(n_blk):
            gm_ = gate[m:m + 1, :]
            beats = (m < blk_of_query) & ((gm_ > gate) | ((gm_ == gate) & (m < blk_of_key)))
            rank = rank + jnp.where(beats, 1.0, 0.0)
        chosen = (blk_of_key < blk_of_query) & (rank < MOBA_TOPK)
        qts.append(qt)
        biases.append(jnp.where(chosen, 0.0, NEG_INF))

    def scores(i):
        return dotf(k_ref[0:(i + 1) * MOBA_BLOCK, :], jnp.concatenate([qt[:, blk(i)] for qt in qts], axis=1))

    def finish(i, h, pu):
        hs = slice(h * HEAD_DIM, (h + 1) * HEAD_DIM)
        n_keys = (i + 1) * MOBA_BLOCK
        ov = dotf(jnp.concatenate([vt_ref[hs, 0:n_keys], ones[:, 0:n_keys]], axis=0), pu)
        o_ref[i, hs, :] = ov[0:HEAD_DIM] * (1.0 / ov[HEAD_DIM:HEAD_DIM + 1])

    order = list(range(n_blk - 1, -1, -1))
    s_next = scores(order[0])
    unfinished = []
    for pos, i in enumerate(order):
        s_pair = s_next
        if pos + 1 < n_blk:
            s_next = scores(order[pos + 1])
        probs = []
        for h in range(n_heads):
            s = s_pair[:, blk(h)]
            bias = biases[h]
            tiles = [s[blk(j)] + bias[j:j + 1, blk(i)] for j in range(i)]
            tiles.append(jnp.where(causal, s[blk(i)], NEG_INF))
            top = tiles[0]
            for t in tiles[1:]:
                top = jnp.maximum(top, t)
            m_q = jnp.max(top, axis=0, keepdims=True)
            probs.append(jnp.concatenate([jnp.exp2(t - m_q).astype(BF16) for t in tiles], axis=0))
        for args in unfinished:
            finish(*args)
        unfinished = [(i, h, probs[h]) for h in range(n_heads)]
    for args in unfinished:
        finish(*args)


def _mix_kernel(x_ref, at_ref, gm_ref, aon_ref, woa_ref, wog_ref, ffn_ref, rwt_ref, rb_ref,
                x1_ref, xs_ref, lpos_ref, gate_ref, tab_ref, tot_ref, carry_ref):
    @pl.when(pl.program_id(0) == 0)
    def _():
        carry_ref[...] = jnp.zeros_like(carry_ref)

    x1 = x_ref[...] + jnp.dot(gm_ref[...], wog_ref[...], preferred_element_type=F32)
    parts = []
    for c in range(TM // MOBA_BLOCK):
        at = at_ref[c]
        ss = jnp.sum(at * at, axis=0, keepdims=True)
        atn = at * lax.rsqrt(ss * (1.0 / ATTN_WIDTH) + NORM_EPS) * aon_ref[...]
        parts.append(atn.T)
    attn = jnp.concatenate(parts, axis=0).astype(BF16)
    x1 = x1 + jnp.dot(attn, woa_ref[...], preferred_element_type=F32)
    x1_ref[...] = x1
    hn = _rms(x1, ffn_ref[...])

    hn_b = hn.astype(BF16)
    hn_lo = (hn - hn_b.astype(F32)).astype(BF16)
    rw = rwt_ref[...]
    rw_hi = rw.astype(BF16)
    rw_lo = (rw - rw_hi.astype(F32)).astype(BF16)
    both = _nt(jnp.concatenate([rw_hi, rw_lo], axis=0), hn_b)
    logits = both[:N_EXPERTS] + both[N_EXPERTS:] + _nt(rw_hi, hn_lo) + rb_ref[...]
    eidx = lax.broadcasted_iota(I32, (N_EXPERTS, TM), 0)
    vals, idxs = [], []
    rest = logits
    for _ in range(TOP_K):
        m = jnp.max(rest, axis=0, keepdims=True)
        idx = jnp.min(jnp.where(rest == m, eidx, N_EXPERTS), axis=0, keepdims=True)
        vals.append(m)
        idxs.append(idx)
        rest = jnp.where(eidx == idx, -jnp.inf, rest)
    exps = [jnp.exp(v - vals[0]) for v in vals]
    denom = exps[0] + exps[1] + exps[2] + exps[3]
    hot = [eidx == idx for idx in idxs]
    multi = jnp.where(hot[0] | hot[1] | hot[2] | hot[3], 1.0, 0.0)

    tp = lax.broadcasted_iota(I32, (TM, TM), 0)
    tq = lax.broadcasted_iota(I32, (TM, TM), 1)
    earlier = jnp.where(tp < tq, 1.0, 0.0).astype(BF16)
    before = jnp.dot(multi.astype(BF16), earlier, preferred_element_type=F32)
    count = jnp.sum(multi, axis=1, keepdims=True)
    seg = jnp.floor((count + (CHUNK - 1)) * (1.0 / CHUNK))
    ep = lax.broadcasted_iota(I32, (N_EXPERTS, N_EXPERTS), 0)
    eq = lax.broadcasted_iota(I32, (N_EXPERTS, N_EXPERTS), 1)
    lower = jnp.where(eq < ep, 1.0, 0.0).astype(BF16)
    seg_wide = jnp.broadcast_to(seg, (N_EXPERTS, LANES))
    off = jnp.dot(lower, seg_wide.astype(BF16), preferred_element_type=F32)
    slot = off[:, 0:1] * CHUNK + before
    lpos = [jnp.sum(jnp.where(hot[kk], slot, 0.0), axis=0, keepdims=True).astype(I32) for kk in range(TOP_K)]
    for kk in range(TOP_K):
        lpos_ref[kk:kk + 1, :] = lpos[kk]
        gate_ref[kk:kk + 1, :] = exps[kk] / denom
    def onehot_rows(c):
        r = lax.broadcasted_iota(I32, (TM, TM), 0) + c * TM
        onehot = jnp.zeros((TM, TM), F32)
        for kk in range(TOP_K):
            onehot = jnp.where(r == lpos[kk], 1.0, onehot)
        return onehot.astype(BF16)

    n_chunks = SORT_ROWS // TM
    nxt = onehot_rows(0)
    for c in range(n_chunks):
        cur = nxt
        if c + 1 < n_chunks:
            nxt = onehot_rows(c + 1)
        xs_ref[c * TM:(c + 1) * TM, :] = jnp.dot(cur, hn_b, preferred_element_type=F32).astype(BF16)
    lane = lax.broadcasted_iota(I32, (N_EXPERTS, LANES), 1)
    tab = jnp.where(lane == 0, seg_wide, jnp.where(lane == 1, off, carry_ref[...]))
    tab_ref[0] = tab.astype(I32)
    carry_ref[...] = carry_ref[...] + seg
    tot_ref[...] = carry_ref[...].astype(I32)


def _chunk_rows(ref, chunk_index):
    return ref.at[pl.ds(pl.multiple_of(chunk_index * CHUNK, CHUNK), CHUNK), :]


def _expert_kernel(blke_ref, nused_ref, wslot_ref, nexte_ref, tab_ref, tab_next_ref, xs_ref, wgu_hbm, bgu_ref,
                   wdn_hbm, bdn_ref, y_ref, xbuf, ybuf, zbuf, wgu_f32, wdn_f32, wgu_bf, wdn_bf, sem_in, sem_out,
                   sem_fill, sem_w):
    i = pl.program_id(0)
    n_used = nused_ref[0]
    used = i < n_used
    slot = i % 2

    def weight_copies(e, s):
        return [pltpu.make_async_copy(wgu_hbm.at[e], wgu_f32.at[s], sem_w.at[s]),
                pltpu.make_async_copy(wdn_hbm.at[e], wdn_f32.at[s], sem_w.at[s])]

    def copies_in(tab, s):
        return [pltpu.make_async_copy(_chunk_rows(xs_ref, tab[0, 0, c]), _chunk_rows(xbuf.at[s], c), sem_in.at[s])
                for c in range(CHUNKS_PER_BLOCK)]

    def copies_out(s):
        return [pltpu.make_async_copy(_chunk_rows(ybuf.at[s], c), _chunk_rows(y_ref, tab_ref[0, 0, CHUNKS_PER_BLOCK + c]),
                                      sem_out.at[s]) for c in range(CHUNKS_PER_BLOCK)]

    @pl.when(i == 0)
    def _():
        zbuf[...] = jnp.zeros_like(zbuf)
        n_regions = y_ref.shape[0] // SORT_ROWS
        tail = SORT_ROWS - PAIRS
        starts = [t * SORT_ROWS + PAIRS for t in range(n_regions)]
        starts += [n_regions * SORT_ROWS + d * tail for d in range(DUMP_ROWS // tail)]
        fills = [pltpu.make_async_copy(zbuf, y_ref.at[pl.ds(r0, tail), :], sem_fill) for r0 in starts]
        for f in fills:
            f.start()
        for f in fills:
            f.wait()
        for cp in copies_in(tab_ref, 0):
            cp.start()
        for cp in weight_copies(blke_ref[0], wslot_ref[0]):
            cp.start()

    @pl.when(used & ((i == 0) | (blke_ref[i] != blke_ref[jnp.maximum(i - 1, 0)])))
    def _():
        ws = wslot_ref[i]
        for cp in weight_copies(blke_ref[i], ws):
            cp.wait()

        @pl.when(nexte_ref[i] >= 0)
        def _():
            for cp in weight_copies(nexte_ref[i], 1 - ws):
                cp.start()

        for c in range(2 * D_EXPERT // FC):
            wgu_bf[:, c * FC:(c + 1) * FC] = wgu_f32[ws, :, c * FC:(c + 1) * FC].astype(BF16)
        for c in range(D_EXPERT // FC):
            wdn_bf[c * FC:(c + 1) * FC, :] = wdn_f32[ws, c * FC:(c + 1) * FC, :].astype(BF16)

    @pl.when(used)
    def _():
        for cp in copies_in(tab_ref, slot):
            cp.wait()

        @pl.when(i >= 2)
        def _():
            for cp in copies_out(slot):
                cp.wait()

        for cp in copies_in(tab_next_ref, 1 - slot):
            cp.start()
        x = xbuf[slot]
        y = jnp.zeros((RB, D_MODEL), F32) + bdn_ref[0]
        for c in range(D_EXPERT // FC):
            g = jnp.dot(x, wgu_bf[:, c * FC:(c + 1) * FC], preferred_element_type=F32) \
                + bgu_ref[0, :, c * FC:(c + 1) * FC]
            lin = jnp.dot(x, wgu_bf[:, D_EXPERT + c * FC:D_EXPERT + (c + 1) * FC],
                          preferred_element_type=F32) + bgu_ref[0, :, D_EXPERT + c * FC:D_EXPERT + (c + 1) * FC]
            g = jnp.minimum(g, SWIGLU_LIMIT)
            lin = jnp.clip(lin, -SWIGLU_LIMIT, SWIGLU_LIMIT)
            a = g * (1.0 / (1.0 + jnp.exp(-SWIGLU_ALPHA * g))) * (lin + 1.0)
            y = y + jnp.dot(a.astype(BF16), wdn_bf[c * FC:(c + 1) * FC, :], preferred_element_type=F32)
        ybuf[slot] = y.astype(BF16)
        for cp in copies_out(slot):
            cp.start()

        @pl.when(i == n_used - 1)
        def _():
            for cp in copies_in(tab_next_ref, 1 - slot):
                cp.wait()
            for cp in copies_out(slot):
                cp.wait()

            @pl.when(i >= 1)
            def _():
                for cp in copies_out(1 - slot):
                    cp.wait()


def _final_kernel(y_ref, x1_ref, lpos_ref, gate_ref, p_ref, plen_ref, wg_ref, wp_ref, post_ref, fin_ref, o_ref):
    lpos = lpos_ref[...]
    gates = gate_ref[...]
    def gate_cols(c):
        r = lax.broadcasted_iota(I32, (TM, TM), 1) + c * TM
        w = jnp.zeros((TM, TM), F32)
        for kk in range(TOP_K):
            w = jnp.where(r == lpos[:, kk:kk + 1], gates[:, kk:kk + 1], w)
        return w.astype(BF16)

    emb = jnp.dot(p_ref[...].astype(BF16), wp_ref[...], preferred_element_type=F32)
    n_chunks = SORT_ROWS // TM
    moe = jnp.zeros((TM, D_MODEL), F32)
    nxt = gate_cols(0)
    for c in range(n_chunks):
        cur = nxt
        if c + 1 < n_chunks:
            nxt = gate_cols(c + 1)
        moe = moe + jnp.dot(cur, y_ref[c * TM:(c + 1) * TM, :], preferred_element_type=F32)
    x2 = x1_ref[...] + moe
    z = jnp.dot(_rms(x2, plen_ref[...]).astype(BF16), wg_ref[...], preferred_element_type=F32)
    ple_gate = 1.0 / (1.0 + jnp.exp(-z))
    x3 = x2 + ple_gate * _rms(emb, post_ref[...])
    o_ref[...] = _rms(x3, fin_ref[...])


def _params(sem):
    return pltpu.CompilerParams(dimension_semantics=sem, vmem_limit_bytes=VMEM_LIMIT)


def _rope_tables(seq):
    half = HEAD_DIM // 2
    inv = ROPE_THETA ** (-jnp.arange(half, dtype=F32) / half)
    ang = jnp.arange(seq, dtype=F32)[:, None] * inv[None, :]
    cos = jnp.cos(ang)
    sin = jnp.sin(ang)
    n_heads = ATTN_WIDTH // HEAD_DIM
    cos_t = jnp.tile(jnp.concatenate([cos, cos], axis=1), (1, n_heads))
    sin_t = jnp.tile(jnp.concatenate([-sin, sin], axis=1), (1, n_heads))
    return cos_t, sin_t


def _layer(x, p, mix_norm, w_in, gmlp_ln_g, gmlp_ln_b, gmlp_ws, gmlp_bs, attn_out_norm, gmlp_out_norm,
           w_out, ffn_norm, router_w, router_b, w_gate_up, b_gate_up, w_down, b_down, ple_norm,
           w_ple_gate, w_ple_proj, ple_post_norm, out_norm):
    B, S, D = x.shape
    T = B * S
    A = ATTN_WIDTH
    n_tiles = T // TM
    n_blk = S // MOBA_BLOCK
    tiles_per_seq = S // TM
    xf = x.reshape(T, D)
    row2 = lambda v: v.reshape(1, -1)
    full = lambda shape: pl.BlockSpec(shape, lambda *_: (0,) * len(shape))

    cos_t, sin_t = _rope_tables(S)
    wk = w_in[:, A:2 * A].astype(BF16)
    wqvt = jnp.concatenate([w_in[:, :A], w_in[:, 2 * A:3 * A]], axis=1).T.astype(BF16)
    wuv = w_in[:, 3 * A:].astype(BF16)

    qt, k, vt, kmean, gm = pl.pallas_call(
        _proj_kernel,
        grid=(n_tiles,),
        in_specs=[pl.BlockSpec((TM, D), lambda i: (i, 0)), full((1, D)), full((D, A)), full((2 * A, D)),
                  full((D, 2 * GMLP_WIDTH)),
                  pl.BlockSpec((TM, A), lambda i: (i % tiles_per_seq, 0)),
                  pl.BlockSpec((TM, A), lambda i: (i % tiles_per_seq, 0)),
                  pl.BlockSpec((A, TM), lambda i: (0, i % tiles_per_seq)),
                  pl.BlockSpec((A, TM), lambda i: (0, i % tiles_per_seq)),
                  full((1, GMLP_WIDTH)), full((1, GMLP_WIDTH)),
                  full((GMLP_GROUPS, GMLP_CHUNK, GMLP_CHUNK)), full((GMLP_GROUPS, GMLP_CHUNK, 1)),
                  full((1, GMLP_WIDTH))],
        out_specs=[pl.BlockSpec((A, TM), lambda i: (0, i)), pl.BlockSpec((TM, A), lambda i: (i, 0)),
                   pl.BlockSpec((A, TM), lambda i: (0, i)),
                   pl.BlockSpec((1, TM // MOBA_BLOCK, A), lambda i: (i, 0, 0)),
                   pl.BlockSpec((TM, GMLP_WIDTH), lambda i: (i, 0))],
        out_shape=[jax.ShapeDtypeStruct((A, T), BF16), jax.ShapeDtypeStruct((T, A), BF16),
                   jax.ShapeDtypeStruct((A, T), BF16),
                   jax.ShapeDtypeStruct((n_tiles, TM // MOBA_BLOCK, A), F32),
                   jax.ShapeDtypeStruct((T, GMLP_WIDTH), BF16)],
        scratch_shapes=[pltpu.VMEM((TM, GMLP_WIDTH), F32)],
        compiler_params=_params(("parallel",)),
        name="proj",
    )(xf, row2(mix_norm), wk, wqvt, wuv, cos_t, sin_t, cos_t.T, sin_t.T, row2(gmlp_ln_g), row2(gmlp_ln_b), gmlp_ws,
      gmlp_bs.reshape(GMLP_GROUPS, GMLP_CHUNK, 1), row2(gmlp_out_norm))

    kmean = kmean.reshape(B, n_blk, A)
    n_pairs = A // LANES
    attn_t = pl.pallas_call(
        _attn_kernel,
        grid=(B, n_pairs),
        in_specs=[pl.BlockSpec((LANES, S), lambda b, h: (h, b)), pl.BlockSpec((S, LANES), lambda b, h: (b, h)),
                  pl.BlockSpec((LANES, S), lambda b, h: (h, b)),
                  pl.BlockSpec((1, n_blk, LANES), lambda b, h: (b, 0, h))],
        out_specs=pl.BlockSpec((n_blk, LANES, MOBA_BLOCK), lambda b, h: (b, h, 0)),
        out_shape=jax.ShapeDtypeStruct((T // MOBA_BLOCK, A, MOBA_BLOCK), F32),
        compiler_params=_params(("parallel", "parallel")),
        name="attn",
    )(qt, k, vt, kmean)

    x1, xs, lpos, gates, tab, tot = pl.pallas_call(
        _mix_kernel,
        grid=(n_tiles,),
        in_specs=[pl.BlockSpec((TM, D), lambda i: (i, 0)),
                  pl.BlockSpec((TM // MOBA_BLOCK, A, MOBA_BLOCK), lambda i: (i, 0, 0)),
                  pl.BlockSpec((TM, GMLP_WIDTH), lambda i: (i, 0)),
                  full((A, 1)), full((A, D)), full((GMLP_WIDTH, D)), full((1, D)),
                  full((N_EXPERTS, D)), full((N_EXPERTS, 1))],
        out_specs=[pl.BlockSpec((TM, D), lambda i: (i, 0)), pl.BlockSpec((SORT_ROWS, D), lambda i: (i, 0)),
                   pl.BlockSpec((TOP_K, TM), lambda i: (0, i)), pl.BlockSpec((TOP_K, TM), lambda i: (0, i)),
                   pl.BlockSpec((1, N_EXPERTS, LANES), lambda i: (i, 0, 0)), full((N_EXPERTS, LANES))],
        out_shape=[jax.ShapeDtypeStruct((T, D), F32), jax.ShapeDtypeStruct((n_tiles * SORT_ROWS, D), BF16),
                   jax.ShapeDtypeStruct((TOP_K, T), I32), jax.ShapeDtypeStruct((TOP_K, T), F32),
                   jax.ShapeDtypeStruct((n_tiles, N_EXPERTS, LANES), I32),
                   jax.ShapeDtypeStruct((N_EXPERTS, LANES), I32)],
        scratch_shapes=[pltpu.VMEM((N_EXPERTS, LANES), F32)],
        compiler_params=_params(("arbitrary",)),
        name="mix",
    )(xf, attn_t, gm, attn_out_norm.reshape(A, 1), w_out[:A].astype(BF16), w_out[A:].astype(BF16),
      row2(ffn_norm), router_w.T, router_b.reshape(N_EXPERTS, 1))

    n_blocks = -(-(T * TOP_K + n_tiles * N_EXPERTS * (CHUNK - 1) + N_EXPERTS * (RB - CHUNK)) // RB)
    seg16, off16, before16 = tab[:, :, 0], tab[:, :, 1], tab[:, :, 2]
    tot16 = tot[:, 0]
    blocks_e = (tot16 + CHUNKS_PER_BLOCK - 1) // CHUNKS_PER_BLOCK
    bend = jnp.cumsum(blocks_e)
    nused = bend[-1:].astype(I32)
    bidx = jnp.arange(n_blocks, dtype=I32)
    blk_e = jnp.minimum(jnp.sum(bend[None, :] <= bidx[:, None], axis=1), N_EXPERTS - 1).astype(I32)
    sel = blk_e[:, None] == jnp.arange(N_EXPERTS, dtype=I32)[None, :]
    pick = lambda v: jnp.sum(jnp.where(sel, v[None, :], 0), axis=1)
    chunk_pos = ((bidx - pick(bend - blocks_e)) * CHUNKS_PER_BLOCK)[:, None] \
        + jnp.arange(CHUNKS_PER_BLOCK, dtype=I32)[None, :]
    valid = (chunk_pos < pick(tot16)[:, None]) & (bidx < nused)[:, None]
    seg_end = jnp.sum(jnp.where(sel[:, None, :], (before16 + seg16)[None], 0), axis=2)
    shift = jnp.sum(jnp.where(sel[:, None, :], (off16 - before16)[None], 0), axis=2)
    tile = jnp.minimum(jnp.sum(seg_end[:, None, :] <= chunk_pos[:, :, None], axis=2), n_tiles - 1)
    in_tile = tile[:, :, None] == jnp.arange(n_tiles, dtype=I32)[None, None, :]
    where16 = tile * CHUNKS_PER_TILE + jnp.sum(jnp.where(in_tile, shift[:, None, :], 0), axis=2) + chunk_pos
    dump16 = (n_tiles * CHUNKS_PER_TILE + (bidx % 2)[:, None] * CHUNKS_PER_BLOCK
              + jnp.arange(CHUNKS_PER_BLOCK, dtype=I32)[None, :])
    tabs = jnp.concatenate([jnp.where(valid, where16, 0), jnp.where(valid, where16, dump16)],
                           axis=1).astype(I32).reshape(n_blocks, 1, 2 * CHUNKS_PER_BLOCK)

    wslot = pick((jnp.cumsum(blocks_e > 0) - 1) % 2).astype(I32)
    next_start = pick(bend)
    next_e = jnp.where(next_start < nused,
                       jnp.minimum(jnp.sum(bend[None, :] <= next_start[:, None], axis=1), N_EXPERTS - 1),
                       -1).astype(I32)

    last = lambda i, nu: jnp.minimum(i, nu[0] - 1)
    smem_tab = lambda step: pl.BlockSpec((1, 1, 2 * CHUNKS_PER_BLOCK),
                                         lambda i, be, nu, ws, ne: (jnp.minimum(i + step, n_blocks - 1), 0, 0),
                                         memory_space=pltpu.SMEM)
    y = pl.pallas_call(
        _expert_kernel,
        grid_spec=pltpu.PrefetchScalarGridSpec(
            num_scalar_prefetch=4, grid=(n_blocks,),
            in_specs=[smem_tab(0), smem_tab(1), pl.BlockSpec(memory_space=pl.ANY),
                      pl.BlockSpec(memory_space=pl.ANY),
                      pl.BlockSpec((1, 1, 2 * D_EXPERT), lambda i, be, nu, ws, ne: (be[last(i, nu)], 0, 0)),
                      pl.BlockSpec(memory_space=pl.ANY),
                      pl.BlockSpec((1, 1, D), lambda i, be, nu, ws, ne: (be[last(i, nu)], 0, 0))],
            out_specs=pl.BlockSpec(memory_space=pl.ANY),
            scratch_shapes=[pltpu.VMEM((2, RB, D), BF16), pltpu.VMEM((2, RB, D), BF16),
                            pltpu.VMEM((SORT_ROWS - PAIRS, D), BF16),
                            pltpu.VMEM((2, D, 2 * D_EXPERT), F32), pltpu.VMEM((2, D_EXPERT, D), F32),
                            pltpu.VMEM((D, 2 * D_EXPERT), BF16), pltpu.VMEM((D_EXPERT, D), BF16),
                            pltpu.SemaphoreType.DMA((2,)), pltpu.SemaphoreType.DMA((2,)),
                            pltpu.SemaphoreType.DMA(()), pltpu.SemaphoreType.DMA((2,))]),
        out_shape=jax.ShapeDtypeStruct((n_tiles * SORT_ROWS + DUMP_ROWS, D), BF16),
        compiler_params=pltpu.CompilerParams(dimension_semantics=("arbitrary",),
                                             vmem_limit_bytes=EXPERT_VMEM_LIMIT),
        name="experts",
    )(blk_e, nused, wslot, next_e, tabs, tabs, xs, w_gate_up, b_gate_up.reshape(N_EXPERTS, 1, -1), w_down,
      b_down.reshape(N_EXPERTS, 1, -1))

    out = pl.pallas_call(
        _final_kernel,
        grid=(n_tiles,),
        in_specs=[pl.BlockSpec((SORT_ROWS, D), lambda i: (i, 0)),
                  pl.BlockSpec((TM, D), lambda i: (i, 0)), pl.BlockSpec((TM, TOP_K), lambda i: (i, 0)),
                  pl.BlockSpec((TM, TOP_K), lambda i: (i, 0)),
                  pl.BlockSpec((TM, PLE_DIM), lambda i: (i, 0)), full((1, D)), full((D, D)), full((PLE_DIM, D)),
                  full((1, D)), full((1, D))],
        out_specs=pl.BlockSpec((TM, D), lambda i: (i, 0)),
        out_shape=jax.ShapeDtypeStruct((T, D), F32),
        compiler_params=_params(("parallel",)),
        name="final",
    )(y, x1, lpos.T, gates.T, p.reshape(T, PLE_DIM), row2(ple_norm), w_ple_gate.astype(BF16),
      w_ple_proj.astype(BF16), row2(ple_post_norm), row2(out_norm))
    return out.reshape(B, S, D)


def kernel(x, p, mix_norm, w_in, gmlp_ln_g, gmlp_ln_b, gmlp_ws, gmlp_bs, attn_out_norm, gmlp_out_norm, w_out, ffn_norm, router_w, router_b, w_gate_up, b_gate_up, w_down, b_down, ple_norm, w_ple_gate, w_ple_proj, ple_post_norm, final_norm):
    depth = p.shape[0]
    assert depth == 1, "the final rmsnorm is fused into the (single) layer"
    return _layer(x, p[0], mix_norm[0], w_in[0], gmlp_ln_g[0], gmlp_ln_b[0], gmlp_ws[0], gmlp_bs[0],
                  attn_out_norm[0], gmlp_out_norm[0], w_out[0], ffn_norm[0], router_w[0], router_b[0],
                  w_gate_up[0], b_gate_up[0], w_down[0], b_down[0], ple_norm[0], w_ple_gate[0],
                  w_ple_proj[0], ple_post_norm[0], final_norm)
```

```python
import jax
import jax.numpy as jnp
import numpy as np
from jax import lax
from jax.experimental import pallas as pl
from jax.experimental.pallas import tpu as pltpu

F32 = jnp.float32
BF16 = jnp.bfloat16
I32 = jnp.int32

D_MODEL = 1024
PLE_DIM = 256
ATTN_WIDTH = 512
HEAD_DIM = 64
ROPE_THETA = 10000.0
MOBA_BLOCK = 256
MOBA_TOPK = 3
GMLP_WIDTH = 512
GMLP_GROUPS = 4
GMLP_GROUP_DIM = 128
GMLP_CHUNK = 128
N_EXPERTS = 32
TOP_K = 4
D_EXPERT = 1024
SWIGLU_LIMIT = 7.0
SWIGLU_ALPHA = 1.702
NORM_EPS = 1e-6
NEG_INF = -1e30
Q_SCALE = float(np.log2(np.e) / np.sqrt(HEAD_DIM))

LANES = 128
CHUNK = 16

TM = 512
RB = 512
FC = 512
PAIRS = TM * TOP_K
SORT_ROWS = 2560
CHUNKS_PER_BLOCK = RB // CHUNK
CHUNKS_PER_TILE = SORT_ROWS // CHUNK
DUMP_ROWS = 2 * RB
VMEM_LIMIT = 48 * 1024 * 1024
EXPERT_VMEM_LIMIT = 56 * 1024 * 1024

assert SORT_ROWS % TM == 0 and SORT_ROWS >= PAIRS + N_EXPERTS * (CHUNK - 1)


def _rms(x, g):
    return x * lax.rsqrt(jnp.mean(x * x, axis=-1, keepdims=True) + NORM_EPS) * g


def _nt(a, b):
    return lax.dot_general(a, b, (((1,), (1,)), ((), ())), preferred_element_type=F32)


def _proj_kernel(x_ref, mixn_ref, wk_ref, wqvt_ref, wuv_ref, cos_ref, sin_ref, cost_ref, sint_ref, lng_ref,
                 lnb_ref, ws_ref, bs_ref, gon_ref, qt_ref, k_ref, vt_ref, kmean_ref, gm_ref, gacc_ref):
    h = _rms(x_ref[...], mixn_ref[...]).astype(BF16)
    half = HEAD_DIM // 2

    uv = jnp.dot(h, wuv_ref[...], preferred_element_type=F32)
    kx = jnp.dot(h, wk_ref[...], preferred_element_type=F32)
    qvt = _nt(wqvt_ref[...], h)
    inv_sqrt2 = np.float32(1.0 / np.sqrt(2.0))

    def gelu(t):
        return 0.5 * t * (1.0 + lax.erf(t * inv_sqrt2))

    gu = gelu(uv[:, :GMLP_WIDTH])
    gv = gelu(uv[:, GMLP_WIDTH:])
    row = lax.broadcasted_iota(I32, (GMLP_CHUNK, GMLP_CHUNK), 0)
    col = lax.broadcasted_iota(I32, (GMLP_CHUNK, GMLP_CHUNK), 1)
    tril = col <= row
    for g in range(GMLP_GROUPS):
        sl = slice(g * GMLP_GROUP_DIM, (g + 1) * GMLP_GROUP_DIM)
        vg = gv[:, sl]
        mu = jnp.mean(vg, axis=-1, keepdims=True)
        dv = vg - mu
        var = jnp.mean(dv * dv, axis=-1, keepdims=True)
        vn = (dv * lax.rsqrt(var + NORM_EPS) * lng_ref[:, sl] + lnb_ref[:, sl]).astype(BF16)
        w = jnp.where(tril, ws_ref[g], 0.0).astype(BF16)
        bias = bs_ref[g]
        for c in range(TM // GMLP_CHUNK):
            rs = slice(c * GMLP_CHUNK, (c + 1) * GMLP_CHUNK)
            mixed = jnp.dot(w, vn[rs], preferred_element_type=F32) + bias
            gacc_ref[rs, sl] = gu[rs, sl] * mixed
    gm_ref[...] = _rms(gacc_ref[...], gon_ref[...]).astype(BF16)

    lane = lax.broadcasted_iota(I32, (TM, LANES), 1)
    first_half = (lane % HEAD_DIM) < half
    parts = []
    for s in range(ATTN_WIDTH // LANES):
        slab = kx[:, s * LANES:(s + 1) * LANES]
        ahead = pltpu.roll(slab, LANES - half, 1)
        behind = pltpu.roll(slab, half, 1)
        parts.append(jnp.where(first_half, ahead, behind))
    k = kx * cos_ref[...] + jnp.concatenate(parts, axis=1) * sin_ref[...]
    k_ref[...] = k.astype(BF16)
    for c in range(TM // MOBA_BLOCK):
        kmean_ref[0, c:c + 1, :] = jnp.mean(k[c * MOBA_BLOCK:(c + 1) * MOBA_BLOCK], axis=0, keepdims=True)

    qx = qvt[:ATTN_WIDTH]
    swapped = []
    for hd in range(ATTN_WIDTH // HEAD_DIM):
        r0 = hd * HEAD_DIM
        swapped += [qx[r0 + half:r0 + HEAD_DIM], qx[r0:r0 + half]]
    qt = (qx * cost_ref[...] + jnp.concatenate(swapped, axis=0) * sint_ref[...]) * Q_SCALE
    qt_ref[...] = qt.astype(BF16)
    vt_ref[...] = qvt[ATTN_WIDTH:].astype(BF16)


def _attn_kernel(qt_ref, k_ref, vt_ref, kmean_ref, o_ref):
    seq = k_ref.shape[0]
    n_blk = seq // MOBA_BLOCK
    feat = lax.broadcasted_iota(I32, (LANES, 1), 0)
    lane = lax.broadcasted_iota(I32, (1, LANES), 1)
    blk_of_key = lax.broadcasted_iota(I32, (n_blk, seq), 0)
    blk_of_query = lax.broadcasted_iota(I32, (n_blk, seq), 1) // MOBA_BLOCK
    key_pos = lax.broadcasted_iota(I32, (MOBA_BLOCK, MOBA_BLOCK), 0)
    query_pos = lax.broadcasted_iota(I32, (MOBA_BLOCK, MOBA_BLOCK), 1)
    causal = key_pos <= query_pos
    blk = lambda i: slice(i * MOBA_BLOCK, (i + 1) * MOBA_BLOCK)
    ones = jnp.ones((16, seq), BF16)

    n_heads = LANES // HEAD_DIM
    dotf = lambda a, b: jnp.dot(a, b, preferred_element_type=F32)
    qts, biases = [], []
    for h in range(n_heads):
        rows_in_head = (feat >= h * HEAD_DIM) & (feat < (h + 1) * HEAD_DIM)
        qt = jnp.where(rows_in_head, qt_ref[...], jnp.zeros((), BF16))

        km = jnp.where((lane >= h * HEAD_DIM) & (lane < (h + 1) * HEAD_DIM), kmean_ref[0], 0.0)
        km1 = km.astype(BF16)
        r1 = km - km1.astype(F32)
        km2 = r1.astype(BF16)
        km3 = (r1 - km2.astype(F32)).astype(BF16)
        gate = dotf(km1, qt) + dotf(km2, qt) + dotf(km3, qt)
        rank = jnp.zeros((n_blk, seq), F32)
        for m in range(n_blk):
            gm_ = gate[m:m + 1, :]
            beats = (m < blk_of_query) & ((gm_ > gate) | ((gm_ == gate) & (m < blk_of_key)))
            rank = rank + jnp.where(beats, 1.0, 0.0)
        chosen = (blk_of_key < blk_of_query) & (rank < MOBA_TOPK)
        qts.append(qt)
        biases.append(jnp.where(chosen, 0.0, NEG_INF))

    def scores(i):
        return dotf(k_ref[0:(i + 1) * MOBA_BLOCK, :], jnp.concatenate([qt[:, blk(i)] for qt in qts], axis=1))

    def finish(i, h, pu):
        hs = slice(h * HEAD_DIM, (h + 1) * HEAD_DIM)
        n_keys = (i + 1) * MOBA_BLOCK
        ov = dotf(jnp.concatenate([vt_ref[hs, 0:n_keys], ones[:, 0:n_keys]], axis=0), pu)
        o_ref[i, hs, :] = ov[0:HEAD_DIM] * (1.0 / ov[HEAD_DIM:HEAD_DIM + 1])

    order = list(range(n_blk - 1, -1, -1))
    s_next = scores(order[0])
    unfinished = []
    for pos, i in enumerate(order):
        s_pair = s_next
        if pos + 1 < n_blk:
            s_next = scores(order[pos + 1])
        probs = []
        for h in range(n_heads):
            s = s_pair[:, blk(h)]
            bias = biases[h]
            tiles = [s[blk(j)] + bias[j:j + 1, blk(i)] for j in range(i)]
            tiles.append(jnp.where(causal, s[blk(i)], NEG_INF))
            top = tiles[0]
            for t in tiles[1:]:
                top = jnp.maximum(top, t)
            m_q = jnp.max(top, axis=0, keepdims=True)
            probs.append(jnp.concatenate([jnp.exp2(t - m_q).astype(BF16) for t in tiles], axis=0))
        for args in unfinished:
            finish(*args)
        unfinished = [(i, h, probs[h]) for h in range(n_heads)]
    for args in unfinished:
        finish(*args)


def _mix_kernel(x_ref, at_ref, gm_ref, aon_ref, woa_ref, wog_ref, ffn_ref, rwt_ref, rb_ref,
                x1_ref, xs_ref, lpos_ref, gate_ref, tab_ref, tot_ref, carry_ref):
    @pl.when(pl.program_id(0) == 0)
    def _():
        carry_ref[...] = jnp.zeros_like(carry_ref)

    x1 = x_ref[...] + jnp.dot(gm_ref[...], wog_ref[...], preferred_element_type=F32)
    parts = []
    for c in range(TM // MOBA_BLOCK):
        at = at_ref[c]
        ss = jnp.sum(at * at, axis=0, keepdims=True)
        atn = at * lax.rsqrt(ss * (1.0 / ATTN_WIDTH) + NORM_EPS) * aon_ref[...]
        parts.append(atn.T)
    attn = jnp.concatenate(parts, axis=0).astype(BF16)
    x1 = x1 + jnp.dot(attn, woa_ref[...], preferred_element_type=F32)
    x1_ref[...] = x1
    hn = _rms(x1, ffn_ref[...])

    hn_b = hn.astype(BF16)
    hn_lo = (hn - hn_b.astype(F32)).astype(BF16)
    rw = rwt_ref[...]
    rw_hi = rw.astype(BF16)
    rw_lo = (rw - rw_hi.astype(F32)).astype(BF16)
    both = _nt(jnp.concatenate([rw_hi, rw_lo], axis=0), hn_b)
    logits = both[:N_EXPERTS] + both[N_EXPERTS:] + _nt(rw_hi, hn_lo) + rb_ref[...]
    eidx = lax.broadcasted_iota(I32, (N_EXPERTS, TM), 0)
    vals, idxs = [], []
    rest = logits
    for _ in range(TOP_K):
        m = jnp.max(rest, axis=0, keepdims=True)
        idx = jnp.min(jnp.where(rest == m, eidx, N_EXPERTS), axis=0, keepdims=True)
        vals.append(m)
        idxs.append(idx)
        rest = jnp.where(eidx == idx, -jnp.inf, rest)
    exps = [jnp.exp(v - vals[0]) for v in vals]
    denom = exps[0] + exps[1] + exps[2] + exps[3]
    hot = [eidx == idx for idx in idxs]
    multi = jnp.where(hot[0] | hot[1] | hot[2] | hot[3], 1.0, 0.0)

    tp = lax.broadcasted_iota(I32, (TM, TM), 0)
    tq = lax.broadcasted_iota(I32, (TM, TM), 1)
    earlier = jnp.where(tp < tq, 1.0, 0.0).astype(BF16)
    before = jnp.dot(multi.astype(BF16), earlier, preferred_element_type=F32)
    count = jnp.sum(multi, axis=1, keepdims=True)
    seg = jnp.floor((count + (CHUNK - 1)) * (1.0 / CHUNK))
    ep = lax.broadcasted_iota(I32, (N_EXPERTS, N_EXPERTS), 0)
    eq = lax.broadcasted_iota(I32, (N_EXPERTS, N_EXPERTS), 1)
    lower = jnp.where(eq < ep, 1.0, 0.0).astype(BF16)
    seg_wide = jnp.broadcast_to(seg, (N_EXPERTS, LANES))
    off = jnp.dot(lower, seg_wide.astype(BF16), preferred_element_type=F32)
    slot = off[:, 0:1] * CHUNK + before
    lpos = [jnp.sum(jnp.where(hot[kk], slot, 0.0), axis=0, keepdims=True).astype(I32) for kk in range(TOP_K)]
    for kk in range(TOP_K):
        lpos_ref[kk:kk + 1, :] = lpos[kk]
        gate_ref[kk:kk + 1, :] = exps[kk] / denom
    def onehot_rows(c):
        r = lax.broadcasted_iota(I32, (TM, TM), 0) + c * TM
        onehot = jnp.zeros((TM, TM), F32)
        for kk in range(TOP_K):
            onehot = jnp.where(r == lpos[kk], 1.0, onehot)
        return onehot.astype(BF16)

    n_chunks = SORT_ROWS // TM
    nxt = onehot_rows(0)
    for c in range(n_chunks):
        cur = nxt
        if c + 1 < n_chunks:
            nxt = onehot_rows(c + 1)
        xs_ref[c * TM:(c + 1) * TM, :] = jnp.dot(cur, hn_b, preferred_element_type=F32).astype(BF16)
    lane = lax.broadcasted_iota(I32, (N_EXPERTS, LANES), 1)
    tab = jnp.where(lane == 0, seg_wide, jnp.where(lane == 1, off, carry_ref[...]))
    tab_ref[0] = tab.astype(I32)
    carry_ref[...] = carry_ref[...] + seg
    tot_ref[...] = carry_ref[...].astype(I32)


def _chunk_rows(ref, chunk_index):
    return ref.at[pl.ds(pl.multiple_of(chunk_index * CHUNK, CHUNK), CHUNK), :]


def _expert_kernel(blke_ref, nused_ref, wslot_ref, nexte_ref, tab_ref, tab_next_ref, xs_ref, wgu_hbm, bgu_ref,
                   wdn_hbm, bdn_ref, y_ref, xbuf, ybuf, zbuf, wgu_f32, wdn_f32, wgu_bf, wdn_bf, sem_in, sem_out,
                   sem_fill, sem_w):
    i = pl.program_id(0)
    n_used = nused_ref[0]
    used = i < n_used
    slot = i % 2

    def weight_copies(e, s):
        return [pltpu.make_async_copy(wgu_hbm.at[e], wgu_f32.at[s], sem_w.at[s]),
                pltpu.make_async_copy(wdn_hbm.at[e], wdn_f32.at[s], sem_w.at[s])]

    def copies_in(tab, s):
        return [pltpu.make_async_copy(_chunk_rows(xs_ref, tab[0, 0, c]), _chunk_rows(xbuf.at[s], c), sem_in.at[s])
                for c in range(CHUNKS_PER_BLOCK)]

    def copies_out(s):
        return [pltpu.make_async_copy(_chunk_rows(ybuf.at[s], c), _chunk_rows(y_ref, tab_ref[0, 0, CHUNKS_PER_BLOCK + c]),
                                      sem_out.at[s]) for c in range(CHUNKS_PER_BLOCK)]

    @pl.when(i == 0)
    def _():
        zbuf[...] = jnp.zeros_like(zbuf)
        n_regions = y_ref.shape[0] // SORT_ROWS
        tail = SORT_ROWS - PAIRS
        starts = [t * SORT_ROWS + PAIRS for t in range(n_regions)]
        starts += [n_regions * SORT_ROWS + d * tail for d in range(DUMP_ROWS // tail)]
        fills = [pltpu.make_async_copy(zbuf, y_ref.at[pl.ds(r0, tail), :], sem_fill) for r0 in starts]
        for f in fills:
            f.start()
        for f in fills:
            f.wait()
        for cp in copies_in(tab_ref, 0):
            cp.start()
        for cp in weight_copies(blke_ref[0], wslot_ref[0]):
            cp.start()

    @pl.when(used & ((i == 0) | (blke_ref[i] != blke_ref[jnp.maximum(i - 1, 0)])))
    def _():
        ws = wslot_ref[i]
        for cp in weight_copies(blke_ref[i], ws):
            cp.wait()

        @pl.when(nexte_ref[i] >= 0)
        def _():
            for cp in weight_copies(nexte_ref[i], 1 - ws):
                cp.start()

        for c in range(2 * D_EXPERT // FC):
            wgu_bf[:, c * FC:(c + 1) * FC] = wgu_f32[ws, :, c * FC:(c + 1) * FC].astype(BF16)
        for c in range(D_EXPERT // FC):
            wdn_bf[c * FC:(c + 1) * FC, :] = wdn_f32[ws, c * FC:(c + 1) * FC, :].astype(BF16)

    @pl.when(used)
    def _():
        for cp in copies_in(tab_ref, slot):
            cp.wait()

        @pl.when(i >= 2)
        def _():
            for cp in copies_out(slot):
                cp.wait()

        x = xbuf[slot]
        y = jnp.zeros((RB, D_MODEL), F32) + bdn_ref[0]
        for c in range(D_EXPERT // FC):
            g = jnp.dot(x, wgu_bf[:, c * FC:(c + 1) * FC], preferred_element_type=F32) \
                + bgu_ref[0, :, c * FC:(c + 1) * FC]
            lin = jnp.dot(x, wgu_bf[:, D_EXPERT + c * FC:D_EXPERT + (c + 1) * FC],
                          preferred_element_type=F32) + bgu_ref[0, :, D_EXPERT + c * FC:D_EXPERT + (c + 1) * FC]
            if c == 0:
                for cp in copies_in(tab_next_ref, 1 - slot):
                    cp.start()
            g = jnp.minimum(g, SWIGLU_LIMIT)
            lin = jnp.clip(lin, -SWIGLU_LIMIT, SWIGLU_LIMIT)
            a = g * (1.0 / (1.0 + jnp.exp(-SWIGLU_ALPHA * g))) * (lin + 1.0)
            y = y + jnp.dot(a.astype(BF16), wdn_bf[c * FC:(c + 1) * FC, :], preferred_element_type=F32)
        ybuf[slot] = y.astype(BF16)
        for cp in copies_out(slot):
            cp.start()

        @pl.when(i == n_used - 1)
        def _():
            for cp in copies_in(tab_next_ref, 1 - slot):
                cp.wait()
            for cp in copies_out(slot):
                cp.wait()

            @pl.when(i >= 1)
            def _():
                for cp in copies_out(1 - slot):
                    cp.wait()


def _final_kernel(y_ref, x1_ref, lpos_ref, gate_ref, p_ref, plen_ref, wg_ref, wp_ref, post_ref, fin_ref, o_ref):
    lpos = lpos_ref[...]
    gates = gate_ref[...]
    def gate_cols(c):
        r = lax.broadcasted_iota(I32, (TM, TM), 1) + c * TM
        w = jnp.zeros((TM, TM), F32)
        for kk in range(TOP_K):
            w = jnp.where(r == lpos[:, kk:kk + 1], gates[:, kk:kk + 1], w)
        return w.astype(BF16)

    emb = jnp.dot(p_ref[...].astype(BF16), wp_ref[...], preferred_element_type=F32)
    n_chunks = SORT_ROWS // TM
    moe = jnp.zeros((TM, D_MODEL), F32)
    nxt = gate_cols(0)
    for c in range(n_chunks):
        cur = nxt
        if c + 1 < n_chunks:
            nxt = gate_cols(c + 1)
        moe = moe + jnp.dot(cur, y_ref[c * TM:(c + 1) * TM, :], preferred_element_type=F32)
    x2 = x1_ref[...] + moe
    z = jnp.dot(_rms(x2, plen_ref[...]).astype(BF16), wg_ref[...], preferred_element_type=F32)
    ple_gate = 1.0 / (1.0 + jnp.exp(-z))
    x3 = x2 + ple_gate * _rms(emb, post_ref[...])
    o_ref[...] = _rms(x3, fin_ref[...])


def _params(sem):
    return pltpu.CompilerParams(dimension_semantics=sem, vmem_limit_bytes=VMEM_LIMIT)


def _rope_tables(seq):
    half = HEAD_DIM // 2
    inv = ROPE_THETA ** (-jnp.arange(half, dtype=F32) / half)
    ang = jnp.arange(seq, dtype=F32)[:, None] * inv[None, :]
    cos = jnp.cos(ang)
    sin = jnp.sin(ang)
    n_heads = ATTN_WIDTH // HEAD_DIM
    cos_t = jnp.tile(jnp.concatenate([cos, cos], axis=1), (1, n_heads))
    sin_t = jnp.tile(jnp.concatenate([-sin, sin], axis=1), (1, n_heads))
    return cos_t, sin_t


def _layer(x, p, mix_norm, w_in, gmlp_ln_g, gmlp_ln_b, gmlp_ws, gmlp_bs, attn_out_norm, gmlp_out_norm,
           w_out, ffn_norm, router_w, router_b, w_gate_up, b_gate_up, w_down, b_down, ple_norm,
           w_ple_gate, w_ple_proj, ple_post_norm, out_norm):
    B, S, D = x.shape
    T = B * S
    A = ATTN_WIDTH
    n_tiles = T // TM
    n_blk = S // MOBA_BLOCK
    tiles_per_seq = S // TM
    xf = x.reshape(T, D)
    row2 = lambda v: v.reshape(1, -1)
    full = lambda shape: pl.BlockSpec(shape, lambda *_: (0,) * len(shape))

    cos_t, sin_t = _rope_tables(S)
    wk = w_in[:, A:2 * A].astype(BF16)
    wqvt = jnp.concatenate([w_in[:, :A], w_in[:, 2 * A:3 * A]], axis=1).T.astype(BF16)
    wuv = w_in[:, 3 * A:].astype(BF16)

    qt, k, vt, kmean, gm = pl.pallas_call(
        _proj_kernel,
        grid=(n_tiles,),
        in_specs=[pl.BlockSpec((TM, D), lambda i: (i, 0)), full((1, D)), full((D, A)), full((2 * A, D)),
                  full((D, 2 * GMLP_WIDTH)),
                  pl.BlockSpec((TM, A), lambda i: (i % tiles_per_seq, 0)),
                  pl.BlockSpec((TM, A), lambda i: (i % tiles_per_seq, 0)),
                  pl.BlockSpec((A, TM), lambda i: (0, i % tiles_per_seq)),
                  pl.BlockSpec((A, TM), lambda i: (0, i % tiles_per_seq)),
                  full((1, GMLP_WIDTH)), full((1, GMLP_WIDTH)),
                  full((GMLP_GROUPS, GMLP_CHUNK, GMLP_CHUNK)), full((GMLP_GROUPS, GMLP_CHUNK, 1)),
                  full((1, GMLP_WIDTH))],
        out_specs=[pl.BlockSpec((A, TM), lambda i: (0, i)), pl.BlockSpec((TM, A), lambda i: (i, 0)),
                   pl.BlockSpec((A, TM), lambda i: (0, i)),
                   pl.BlockSpec((1, TM // MOBA_BLOCK, A), lambda i: (i, 0, 0)),
                   pl.BlockSpec((TM, GMLP_WIDTH), lambda i: (i, 0))],
        out_shape=[jax.ShapeDtypeStruct((A, T), BF16), jax.ShapeDtypeStruct((T, A), BF16),
                   jax.ShapeDtypeStruct((A, T), BF16),
                   jax.ShapeDtypeStruct((n_tiles, TM // MOBA_BLOCK, A), F32),
                   jax.ShapeDtypeStruct((T, GMLP_WIDTH), BF16)],
        scratch_shapes=[pltpu.VMEM((TM, GMLP_WIDTH), F32)],
        compiler_params=_params(("parallel",)),
        name="proj",
    )(xf, row2(mix_norm), wk, wqvt, wuv, cos_t, sin_t, cos_t.T, sin_t.T, row2(gmlp_ln_g), row2(gmlp_ln_b), gmlp_ws,
      gmlp_bs.reshape(GMLP_GROUPS, GMLP_CHUNK, 1), row2(gmlp_out_norm))

    kmean = kmean.reshape(B, n_blk, A)
    n_pairs = A // LANES
    attn_t = pl.pallas_call(
        _attn_kernel,
        grid=(B, n_pairs),
        in_specs=[pl.BlockSpec((LANES, S), lambda b, h: (h, b)), pl.BlockSpec((S, LANES), lambda b, h: (b, h)),
                  pl.BlockSpec((LANES, S), lambda b, h: (h, b)),
                  pl.BlockSpec((1, n_blk, LANES), lambda b, h: (b, 0, h))],
        out_specs=pl.BlockSpec((n_blk, LANES, MOBA_BLOCK), lambda b, h: (b, h, 0)),
        out_shape=jax.ShapeDtypeStruct((T // MOBA_BLOCK, A, MOBA_BLOCK), F32),
        compiler_params=_params(("parallel", "parallel")),
        name="attn",
    )(qt, k, vt, kmean)

    x1, xs, lpos, gates, tab, tot = pl.pallas_call(
        _mix_kernel,
        grid=(n_tiles,),
        in_specs=[pl.BlockSpec((TM, D), lambda i: (i, 0)),
                  pl.BlockSpec((TM // MOBA_BLOCK, A, MOBA_BLOCK), lambda i: (i, 0, 0)),
                  pl.BlockSpec((TM, GMLP_WIDTH), lambda i: (i, 0)),
                  full((A, 1)), full((A, D)), full((GMLP_WIDTH, D)), full((1, D)),
                  full((N_EXPERTS, D)), full((N_EXPERTS, 1))],
        out_specs=[pl.BlockSpec((TM, D), lambda i: (i, 0)), pl.BlockSpec((SORT_ROWS, D), lambda i: (i, 0)),
                   pl.BlockSpec((TOP_K, TM), lambda i: (0, i)), pl.BlockSpec((TOP_K, TM), lambda i: (0, i)),
                   pl.BlockSpec((1, N_EXPERTS, LANES), lambda i: (i, 0, 0)), full((N_EXPERTS, LANES))],
        out_shape=[jax.ShapeDtypeStruct((T, D), F32), jax.ShapeDtypeStruct((n_tiles * SORT_ROWS, D), BF16),
                   jax.ShapeDtypeStruct((TOP_K, T), I32), jax.ShapeDtypeStruct((TOP_K, T), F32),
                   jax.ShapeDtypeStruct((n_tiles, N_EXPERTS, LANES), I32),
                   jax.ShapeDtypeStruct((N_EXPERTS, LANES), I32)],
        scratch_shapes=[pltpu.VMEM((N_EXPERTS, LANES), F32)],
        compiler_params=_params(("arbitrary",)),
        name="mix",
    )(xf, attn_t, gm, attn_out_norm.reshape(A, 1), w_out[:A].astype(BF16), w_out[A:].astype(BF16),
      row2(ffn_norm), router_w.T, router_b.reshape(N_EXPERTS, 1))

    n_blocks = -(-(T * TOP_K + n_tiles * N_EXPERTS * (CHUNK - 1) + N_EXPERTS * (RB - CHUNK)) // RB)
    seg16, off16, before16 = tab[:, :, 0], tab[:, :, 1], tab[:, :, 2]
    tot16 = tot[:, 0]
    blocks_e = (tot16 + CHUNKS_PER_BLOCK - 1) // CHUNKS_PER_BLOCK
    bend = jnp.cumsum(blocks_e)
    nused = bend[-1:].astype(I32)
    bidx = jnp.arange(n_blocks, dtype=I32)
    blk_e = jnp.minimum(jnp.sum(bend[None, :] <= bidx[:, None], axis=1), N_EXPERTS - 1).astype(I32)
    sel = blk_e[:, None] == jnp.arange(N_EXPERTS, dtype=I32)[None, :]
    pick = lambda v: jnp.sum(jnp.where(sel, v[None, :], 0), axis=1)
    chunk_pos = ((bidx - pick(bend - blocks_e)) * CHUNKS_PER_BLOCK)[:, None] \
        + jnp.arange(CHUNKS_PER_BLOCK, dtype=I32)[None, :]
    valid = (chunk_pos < pick(tot16)[:, None]) & (bidx < nused)[:, None]
    seg_end = jnp.sum(jnp.where(sel[:, None, :], (before16 + seg16)[None], 0), axis=2)
    shift = jnp.sum(jnp.where(sel[:, None, :], (off16 - before16)[None], 0), axis=2)
    tile = jnp.minimum(jnp.sum(seg_end[:, None, :] <= chunk_pos[:, :, None], axis=2), n_tiles - 1)
    in_tile = tile[:, :, None] == jnp.arange(n_tiles, dtype=I32)[None, None, :]
    where16 = tile * CHUNKS_PER_TILE + jnp.sum(jnp.where(in_tile, shift[:, None, :], 0), axis=2) + chunk_pos
    dump16 = (n_tiles * CHUNKS_PER_TILE + (bidx % 2)[:, None] * CHUNKS_PER_BLOCK
              + jnp.arange(CHUNKS_PER_BLOCK, dtype=I32)[None, :])
    tabs = jnp.concatenate([jnp.where(valid, where16, 0), jnp.where(valid, where16, dump16)],
                           axis=1).astype(I32).reshape(n_blocks, 1, 2 * CHUNKS_PER_BLOCK)

    wslot = pick((jnp.cumsum(blocks_e > 0) - 1) % 2).astype(I32)
    next_start = pick(bend)
    next_e = jnp.where(next_start < nused,
                       jnp.minimum(jnp.sum(bend[None, :] <= next_start[:, None], axis=1), N_EXPERTS - 1),
                       -1).astype(I32)

    last = lambda i, nu: jnp.minimum(i, nu[0] - 1)
    smem_tab = lambda step: pl.BlockSpec((1, 1, 2 * CHUNKS_PER_BLOCK),
                                         lambda i, be, nu, ws, ne: (jnp.minimum(i + step, n_blocks - 1), 0, 0),
                                         memory_space=pltpu.SMEM)
    y = pl.pallas_call(
        _expert_kernel,
        grid_spec=pltpu.PrefetchScalarGridSpec(
            num_scalar_prefetch=4, grid=(n_blocks,),
            in_specs=[smem_tab(0), smem_tab(1), pl.BlockSpec(memory_space=pl.ANY),
                      pl.BlockSpec(memory_space=pl.ANY),
                      pl.BlockSpec((1, 1, 2 * D_EXPERT), lambda i, be, nu, ws, ne: (be[last(i, nu)], 0, 0)),
                      pl.BlockSpec(memory_space=pl.ANY),
                      pl.BlockSpec((1, 1, D), lambda i, be, nu, ws, ne: (be[last(i, nu)], 0, 0))],
            out_specs=pl.BlockSpec(memory_space=pl.ANY),
            scratch_shapes=[pltpu.VMEM((2, RB, D), BF16), pltpu.VMEM((2, RB, D), BF16),
                            pltpu.VMEM((SORT_ROWS - PAIRS, D), BF16),
                            pltpu.VMEM((2, D, 2 * D_EXPERT), F32), pltpu.VMEM((2, D_EXPERT, D), F32),
                            pltpu.VMEM((D, 2 * D_EXPERT), BF16), pltpu.VMEM((D_EXPERT, D), BF16),
                            pltpu.SemaphoreType.DMA((2,)), pltpu.SemaphoreType.DMA((2,)),
                            pltpu.SemaphoreType.DMA(()), pltpu.SemaphoreType.DMA((2,))]),
        out_shape=jax.ShapeDtypeStruct((n_tiles * SORT_ROWS + DUMP_ROWS, D), BF16),
        compiler_params=pltpu.CompilerParams(dimension_semantics=("arbitrary",),
                                             vmem_limit_bytes=EXPERT_VMEM_LIMIT),
        name="experts",
    )(blk_e, nused, wslot, next_e, tabs, tabs, xs, w_gate_up, b_gate_up.reshape(N_EXPERTS, 1, -1), w_down,
      b_down.reshape(N_EXPERTS, 1, -1))

    out = pl.pallas_call(
        _final_kernel,
        grid=(n_tiles,),
        in_specs=[pl.BlockSpec((SORT_ROWS, D), lambda i: (i, 0)),
                  pl.BlockSpec((TM, D), lambda i: (i, 0)), pl.BlockSpec((TM, TOP_K), lambda i: (i, 0)),
                  pl.BlockSpec((TM, TOP_K), lambda i: (i, 0)),
                  pl.BlockSpec((TM, PLE_DIM), lambda i: (i, 0)), full((1, D)), full((D, D)), full((PLE_DIM, D)),
                  full((1, D)), full((1, D))],
        out_specs=pl.BlockSpec((TM, D), lambda i: (i, 0)),
        out_shape=jax.ShapeDtypeStruct((T, D), F32),
        compiler_params=_params(("parallel",)),
        name="final",
    )(y, x1, lpos.T, gates.T, p.reshape(T, PLE_DIM), row2(ple_norm), w_ple_gate.astype(BF16),
      w_ple_proj.astype(BF16), row2(ple_post_norm), row2(out_norm))
    return out.reshape(B, S, D)


def kernel(x, p, mix_norm, w_in, gmlp_ln_g, gmlp_ln_b, gmlp_ws, gmlp_bs, attn_out_norm, gmlp_out_norm, w_out, ffn_norm, router_w, router_b, w_gate_up, b_gate_up, w_down, b_down, ple_norm, w_ple_gate, w_ple_proj, ple_post_norm, final_norm):
    depth = p.shape[0]
    assert depth == 1, "the final rmsnorm is fused into the (single) layer"
    return _layer(x, p[0], mix_norm[0], w_in[0], gmlp_ln_g[0], gmlp_ln_b[0], gmlp_ws[0], gmlp_bs[0],
                  attn_out_norm[0], gmlp_out_norm[0], w_out[0], ffn_norm[0], router_w[0], router_b[0],
                  w_gate_up[0], b_gate_up[0], w_down[0], b_down[0], ple_norm[0], w_ple_gate[0],
                  w_ple_proj[0], ple_post_norm[0], final_norm)
```

```python
import jax
import jax.numpy as jnp
import numpy as np
from jax import lax
from jax.experimental import pallas as pl
from jax.experimental.pallas import tpu as pltpu

F32 = jnp.float32
BF16 = jnp.bfloat16
I32 = jnp.int32

D_MODEL = 1024
PLE_DIM = 256
ATTN_WIDTH = 512
HEAD_DIM = 64
ROPE_THETA = 10000.0
MOBA_BLOCK = 256
MOBA_TOPK = 3
GMLP_WIDTH = 512
GMLP_GROUPS = 4
GMLP_GROUP_DIM = 128
GMLP_CHUNK = 128
N_EXPERTS = 32
TOP_K = 4
D_EXPERT = 1024
SWIGLU_LIMIT = 7.0
SWIGLU_ALPHA = 1.702
NORM_EPS = 1e-6
NEG_INF = -1e30
Q_SCALE = float(np.log2(np.e) / np.sqrt(HEAD_DIM))

LANES = 128
CHUNK = 16

TM = 512
RB = 512
FC = 512
PAIRS = TM * TOP_K
SORT_ROWS = 2560
CHUNKS_PER_BLOCK = RB // CHUNK
CHUNKS_PER_TILE = SORT_ROWS // CHUNK
DUMP_ROWS = 2 * RB
VMEM_LIMIT = 48 * 1024 * 1024
EXPERT_VMEM_LIMIT = 56 * 1024 * 1024

assert SORT_ROWS % TM == 0 and SORT_ROWS >= PAIRS + N_EXPERTS * (CHUNK - 1)


def _rms(x, g):
    return x * lax.rsqrt(jnp.mean(x * x, axis=-1, keepdims=True) + NORM_EPS) * g


def _nt(a, b):
    return lax.dot_general(a, b, (((1,), (1,)), ((), ())), preferred_element_type=F32)


def _proj_kernel(x_ref, mixn_ref, wk_ref, wqvt_ref, wuv_ref, cos_ref, sin_ref, cost_ref, sint_ref, lng_ref,
                 lnb_ref, ws_ref, bs_ref, gon_ref, qt_ref, k_ref, vt_ref, kmean_ref, gm_ref, gacc_ref):
    h = _rms(x_ref[...], mixn_ref[...]).astype(BF16)
    half = HEAD_DIM // 2

    uv = jnp.dot(h, wuv_ref[...], preferred_element_type=F32)
    kx = jnp.dot(h, wk_ref[...], preferred_element_type=F32)
    qvt = _nt(wqvt_ref[...], h)
    inv_sqrt2 = np.float32(1.0 / np.sqrt(2.0))

    def gelu(t):
        return 0.5 * t * (1.0 + lax.erf(t * inv_sqrt2))

    gu = gelu(uv[:, :GMLP_WIDTH])
    gv = gelu(uv[:, GMLP_WIDTH:])
    row = lax.broadcasted_iota(I32, (GMLP_CHUNK, GMLP_CHUNK), 0)
    col = lax.broadcasted_iota(I32, (GMLP_CHUNK, GMLP_CHUNK), 1)
    tril = col <= row
    for g in range(GMLP_GROUPS):
        sl = slice(g * GMLP_GROUP_DIM, (g + 1) * GMLP_GROUP_DIM)
        vg = gv[:, sl]
        mu = jnp.mean(vg, axis=-1, keepdims=True)
        dv = vg - mu
        var = jnp.mean(dv * dv, axis=-1, keepdims=True)
        vn = (dv * lax.rsqrt(var + NORM_EPS) * lng_ref[:, sl] + lnb_ref[:, sl]).astype(BF16)
        w = jnp.where(tril, ws_ref[g], 0.0).astype(BF16)
        bias = bs_ref[g]
        for c in range(TM // GMLP_CHUNK):
            rs = slice(c * GMLP_CHUNK, (c + 1) * GMLP_CHUNK)
            mixed = jnp.dot(w, vn[rs], preferred_element_type=F32) + bias
            gacc_ref[rs, sl] = gu[rs, sl] * mixed
    gm_ref[...] = _rms(gacc_ref[...], gon_ref[...]).astype(BF16)

    lane = lax.broadcasted_iota(I32, (TM, LANES), 1)
    first_half = (lane % HEAD_DIM) < half
    parts = []
    for s in range(ATTN_WIDTH // LANES):
        slab = kx[:, s * LANES:(s + 1) * LANES]
        ahead = pltpu.roll(slab, LANES - half, 1)
        behind = pltpu.roll(slab, half, 1)
        parts.append(jnp.where(first_half, ahead, behind))
    k = kx * cos_ref[...] + jnp.concatenate(parts, axis=1) * sin_ref[...]
    k_ref[...] = k.astype(BF16)
    for c in range(TM // MOBA_BLOCK):
        kmean_ref[0, c:c + 1, :] = jnp.mean(k[c * MOBA_BLOCK:(c + 1) * MOBA_BLOCK], axis=0, keepdims=True)

    qx = qvt[:ATTN_WIDTH]
    swapped = []
    for hd in range(ATTN_WIDTH // HEAD_DIM):
        r0 = hd * HEAD_DIM
        swapped += [qx[r0 + half:r0 + HEAD_DIM], qx[r0:r0 + half]]
    qt = (qx * cost_ref[...] + jnp.concatenate(swapped, axis=0) * sint_ref[...]) * Q_SCALE
    qt_ref[...] = qt.astype(BF16)
    vt_ref[...] = qvt[ATTN_WIDTH:].astype(BF16)


def _attn_kernel(qt_ref, k_ref, vt_ref, kmean_ref, o_ref):
    seq = k_ref.shape[0]
    n_blk = seq // MOBA_BLOCK
    feat = lax.broadcasted_iota(I32, (LANES, 1), 0)
    lane = lax.broadcasted_iota(I32, (1, LANES), 1)
    blk_of_key = lax.broadcasted_iota(I32, (n_blk, seq), 0)
    blk_of_query = lax.broadcasted_iota(I32, (n_blk, seq), 1) // MOBA_BLOCK
    key_pos = lax.broadcasted_iota(I32, (MOBA_BLOCK, MOBA_BLOCK), 0)
    query_pos = lax.broadcasted_iota(I32, (MOBA_BLOCK, MOBA_BLOCK), 1)
    causal = key_pos <= query_pos
    blk = lambda i: slice(i * MOBA_BLOCK, (i + 1) * MOBA_BLOCK)
    ones = jnp.ones((16, seq), BF16)

    n_heads = LANES // HEAD_DIM
    dotf = lambda a, b: jnp.dot(a, b, preferred_element_type=F32)
    qts, biases = [], []
    for h in range(n_heads):
        rows_in_head = (feat >= h * HEAD_DIM) & (feat < (h + 1) * HEAD_DIM)
        qt = jnp.where(rows_in_head, qt_ref[...], jnp.zeros((), BF16))

        km = jnp.where((lane >= h * HEAD_DIM) & (lane < (h + 1) * HEAD_DIM), kmean_ref[0], 0.0)
        km1 = km.astype(BF16)
        r1 = km - km1.astype(F32)
        km2 = r1.astype(BF16)
        km3 = (r1 - km2.astype(F32)).astype(BF16)
        gate = dotf(km1, qt) + dotf(km2, qt) + dotf(km3, qt)
        rank = jnp.zeros((n_blk, seq), F32)
        for m in range(n_blk):
            gm_ = gate[m:m + 1, :]
            beats = (m < blk_of_query) & ((gm_ > gate) | ((gm_ == gate) & (m < blk_of_key)))
            rank = rank + jnp.where(beats, 1.0, 0.0)
        chosen = (blk_of_key < blk_of_query) & (rank < MOBA_TOPK)
        qts.append(qt)
        biases.append(jnp.where(chosen, 0.0, NEG_INF))

    def scores(i):
        return dotf(k_ref[0:(i + 1) * MOBA_BLOCK, :], jnp.concatenate([qt[:, blk(i)] for qt in qts], axis=1))

    def finish(i, h, pu):
        hs = slice(h * HEAD_DIM, (h + 1) * HEAD_DIM)
        n_keys = (i + 1) * MOBA_BLOCK
        ov = dotf(jnp.concatenate([vt_ref[hs, 0:n_keys], ones[:, 0:n_keys]], axis=0), pu)
        o_ref[i, hs, :] = ov[0:HEAD_DIM] * (1.0 / ov[HEAD_DIM:HEAD_DIM + 1])

    order = list(range(n_blk - 1, -1, -1))
    s_next = scores(order[0])
    unfinished = []
    for pos, i in enumerate(order):
        s_pair = s_next
        if pos + 1 < n_blk:
            s_next = scores(order[pos + 1])
        probs = []
        for h in range(n_heads):
            s = s_pair[:, blk(h)]
            bias = biases[h]
            tiles = [s[blk(j)] + bias[j:j + 1, blk(i)] for j in range(i)]
            tiles.append(jnp.where(causal, s[blk(i)], NEG_INF))
            top = tiles[0]
            for t in tiles[1:]:
                top = jnp.maximum(top, t)
            m_q = jnp.max(top, axis=0, keepdims=True)
            probs.append(jnp.concatenate([jnp.exp2(t - m_q).astype(BF16) for t in tiles], axis=0))
        for args in unfinished:
            finish(*args)
        unfinished = [(i, h, probs[h]) for h in range(n_heads)]
    for args in unfinished:
        finish(*args)


def _mix_kernel(x_ref, at_ref, gm_ref, aon_ref, woa_ref, wog_ref, ffn_ref, rwt_ref, rb_ref,
                x1_ref, xs_ref, lpos_ref, gate_ref, tab_ref, tot_ref, carry_ref):
    @pl.when(pl.program_id(0) == 0)
    def _():
        carry_ref[...] = jnp.zeros_like(carry_ref)

    x1 = x_ref[...] + jnp.dot(gm_ref[...], wog_ref[...], preferred_element_type=F32)
    parts = []
    for c in range(TM // MOBA_BLOCK):
        at = at_ref[c]
        ss = jnp.sum(at * at, axis=0, keepdims=True)
        atn = at * lax.rsqrt(ss * (1.0 / ATTN_WIDTH) + NORM_EPS) * aon_ref[...]
        parts.append(atn.T)
    attn = jnp.concatenate(parts, axis=0).astype(BF16)
    x1 = x1 + jnp.dot(attn, woa_ref[...], preferred_element_type=F32)
    x1_ref[...] = x1
    hn = _rms(x1, ffn_ref[...])

    hn_b = hn.astype(BF16)
    hn_lo = (hn - hn_b.astype(F32)).astype(BF16)
    rw = rwt_ref[...]
    rw_hi = rw.astype(BF16)
    rw_lo = (rw - rw_hi.astype(F32)).astype(BF16)
    both = _nt(jnp.concatenate([rw_hi, rw_lo], axis=0), hn_b)
    logits = both[:N_EXPERTS] + both[N_EXPERTS:] + _nt(rw_hi, hn_lo) + rb_ref[...]
    eidx = lax.broadcasted_iota(I32, (N_EXPERTS, TM), 0)
    vals, idxs = [], []
    rest = logits
    for _ in range(TOP_K):
        m = jnp.max(rest, axis=0, keepdims=True)
        idx = jnp.min(jnp.where(rest == m, eidx, N_EXPERTS), axis=0, keepdims=True)
        vals.append(m)
        idxs.append(idx)
        rest = jnp.where(eidx == idx, -jnp.inf, rest)
    exps = [jnp.exp(v - vals[0]) for v in vals]
    denom = exps[0] + exps[1] + exps[2] + exps[3]
    hot = [eidx == idx for idx in idxs]
    multi = jnp.where(hot[0] | hot[1] | hot[2] | hot[3], 1.0, 0.0)

    tp = lax.broadcasted_iota(I32, (TM, TM), 0)
    tq = lax.broadcasted_iota(I32, (TM, TM), 1)
    earlier = jnp.where(tp < tq, 1.0, 0.0).astype(BF16)
    before = jnp.dot(multi.astype(BF16), earlier, preferred_element_type=F32)
    count = jnp.sum(multi, axis=1, keepdims=True)
    seg = jnp.floor((count + (CHUNK - 1)) * (1.0 / CHUNK))
    ep = lax.broadcasted_iota(I32, (N_EXPERTS, N_EXPERTS), 0)
    eq = lax.broadcasted_iota(I32, (N_EXPERTS, N_EXPERTS), 1)
    lower = jnp.where(eq < ep, 1.0, 0.0).astype(BF16)
    seg_wide = jnp.broadcast_to(seg, (N_EXPERTS, LANES))
    off = jnp.dot(lower, seg_wide.astype(BF16), preferred_element_type=F32)
    slot = off[:, 0:1] * CHUNK + before
    lpos = [jnp.sum(jnp.where(hot[kk], slot, 0.0), axis=0, keepdims=True).astype(I32) for kk in range(TOP_K)]
    for kk in range(TOP_K):
        lpos_ref[kk:kk + 1, :] = lpos[kk]
        gate_ref[kk:kk + 1, :] = exps[kk] / denom
    def onehot_rows(c):
        r = lax.broadcasted_iota(I32, (TM, TM), 0) + c * TM
        onehot = jnp.zeros((TM, TM), F32)
        for kk in range(TOP_K):
            onehot = jnp.where(r == lpos[kk], 1.0, onehot)
        return onehot.astype(BF16)

    n_chunks = SORT_ROWS // TM
    nxt = onehot_rows(0)
    for c in range(n_chunks):
        cur = nxt
        if c + 1 < n_chunks:
            nxt = onehot_rows(c + 1)
        xs_ref[c * TM:(c + 1) * TM, :] = jnp.dot(cur, hn_b, preferred_element_type=F32).astype(BF16)
    lane = lax.broadcasted_iota(I32, (N_EXPERTS, LANES), 1)
    tab = jnp.where(lane == 0, seg_wide, jnp.where(lane == 1, off, carry_ref[...]))
    tab_ref[0] = tab.astype(I32)
    carry_ref[...] = carry_ref[...] + seg
    tot_ref[...] = carry_ref[...].astype(I32)


def _chunk_rows(ref, chunk_index):
    return ref.at[pl.ds(pl.multiple_of(chunk_index * CHUNK, CHUNK), CHUNK), :]


def _expert_kernel(blke_ref, nused_ref, wslot_ref, nexte_ref, tab_ref, tab_next_ref, xs_ref, wgu_hbm, bgu_ref,
                   wdn_hbm, bdn_ref, y_ref, xbuf, ybuf, zbuf, wgu_f32, wdn_f32, wgu_bf, wdn_bf, sem_in, sem_out,
                   sem_fill, sem_w):
    i = pl.program_id(0)
    n_used = nused_ref[0]
    used = i < n_used
    slot = i % 2

    def weight_copies(e, s):
        return [pltpu.make_async_copy(wgu_hbm.at[e], wgu_f32.at[s], sem_w.at[s]),
                pltpu.make_async_copy(wdn_hbm.at[e], wdn_f32.at[s], sem_w.at[s])]

    def copies_in(tab, s):
        return [pltpu.make_async_copy(_chunk_rows(xs_ref, tab[0, 0, c]), _chunk_rows(xbuf.at[s], c), sem_in.at[s])
                for c in range(CHUNKS_PER_BLOCK)]

    def copies_out(s):
        return [pltpu.make_async_copy(_chunk_rows(ybuf.at[s], c), _chunk_rows(y_ref, tab_ref[0, 0, CHUNKS_PER_BLOCK + c]),
                                      sem_out.at[s]) for c in range(CHUNKS_PER_BLOCK)]

    @pl.when(i == 0)
    def _():
        zbuf[...] = jnp.zeros_like(zbuf)
        n_regions = y_ref.shape[0] // SORT_ROWS
        tail = SORT_ROWS - PAIRS
        starts = [t * SORT_ROWS + PAIRS for t in range(n_regions)]
        starts += [n_regions * SORT_ROWS + d * tail for d in range(DUMP_ROWS // tail)]
        fills = [pltpu.make_async_copy(zbuf, y_ref.at[pl.ds(r0, tail), :], sem_fill) for r0 in starts]
        for f in fills:
            f.start()
        for f in fills:
            f.wait()
        for cp in copies_in(tab_ref, 0):
            cp.start()
        for cp in weight_copies(blke_ref[0], wslot_ref[0]):
            cp.start()

    @pl.when(used & ((i == 0) | (blke_ref[i] != blke_ref[jnp.maximum(i - 1, 0)])))
    def _():
        ws = wslot_ref[i]
        for cp in weight_copies(blke_ref[i], ws):
            cp.wait()

        @pl.when(nexte_ref[i] >= 0)
        def _():
            for cp in weight_copies(nexte_ref[i], 1 - ws):
                cp.start()

        for c in range(2 * D_EXPERT // FC):
            wgu_bf[:, c * FC:(c + 1) * FC] = wgu_f32[ws, :, c * FC:(c + 1) * FC].astype(BF16)
        for c in range(D_EXPERT // FC):
            wdn_bf[c * FC:(c + 1) * FC, :] = wdn_f32[ws, c * FC:(c + 1) * FC, :].astype(BF16)

    @pl.when(used)
    def _():
        for cp in copies_in(tab_next_ref, 1 - slot):
            cp.start()
        for cp in copies_in(tab_ref, slot):
            cp.wait()

        @pl.when(i >= 2)
        def _():
            for cp in copies_out(slot):
                cp.wait()

        x = xbuf[slot]
        y = jnp.zeros((RB, D_MODEL), F32) + bdn_ref[0]
        for c in range(D_EXPERT // FC):
            g = jnp.dot(x, wgu_bf[:, c * FC:(c + 1) * FC], preferred_element_type=F32) \
                + bgu_ref[0, :, c * FC:(c + 1) * FC]
            lin = jnp.dot(x, wgu_bf[:, D_EXPERT + c * FC:D_EXPERT + (c + 1) * FC],
                          preferred_element_type=F32) + bgu_ref[0, :, D_EXPERT + c * FC:D_EXPERT + (c + 1) * FC]
            g = jnp.minimum(g, SWIGLU_LIMIT)
            lin = jnp.clip(lin, -SWIGLU_LIMIT, SWIGLU_LIMIT)
            a = g * (1.0 / (1.0 + jnp.exp(-SWIGLU_ALPHA * g))) * (lin + 1.0)
            y = y + jnp.dot(a.astype(BF16), wdn_bf[c * FC:(c + 1) * FC, :], preferred_element_type=F32)
        ybuf[slot] = y.astype(BF16)
        for cp in copies_out(slot):
            cp.start()

        @pl.when(i == n_used - 1)
        def _():
            for cp in copies_in(tab_next_ref, 1 - slot):
                cp.wait()
            for cp in copies_out(slot):
                cp.wait()

            @pl.when(i >= 1)
            def _():
                for cp in copies_out(1 - slot):
                    cp.wait()


def _final_kernel(y_ref, x1_ref, lpos_ref, gate_ref, p_ref, plen_ref, wg_ref, wp_ref, post_ref, fin_ref, o_ref):
    lpos = lpos_ref[...]
    gates = gate_ref[...]
    def gate_cols(c):
        r = lax.broadcasted_iota(I32, (TM, TM), 1) + c * TM
        w = jnp.zeros((TM, TM), F32)
        for kk in range(TOP_K):
            w = jnp.where(r == lpos[:, kk:kk + 1], gates[:, kk:kk + 1], w)
        return w.astype(BF16)

    emb = jnp.dot(p_ref[...].astype(BF16), wp_ref[...], preferred_element_type=F32)
    n_chunks = SORT_ROWS // TM
    moe = jnp.zeros((TM, D_MODEL), F32)
    nxt = gate_cols(0)
    for c in range(n_chunks):
        cur = nxt
        if c + 1 < n_chunks:
            nxt = gate_cols(c + 1)
        moe = moe + jnp.dot(cur, y_ref[c * TM:(c + 1) * TM, :], preferred_element_type=F32)
    x2 = x1_ref[...] + moe
    z = jnp.dot(_rms(x2, plen_ref[...]).astype(BF16), wg_ref[...], preferred_element_type=F32)
    ple_gate = 1.0 / (1.0 + jnp.exp(-z))
    x3 = x2 + ple_gate * _rms(emb, post_ref[...])
    o_ref[...] = _rms(x3, fin_ref[...])


def _params(sem):
    return pltpu.CompilerParams(dimension_semantics=sem, vmem_limit_bytes=VMEM_LIMIT)


def _rope_tables(seq):
    half = HEAD_DIM // 2
    inv = ROPE_THETA ** (-jnp.arange(half, dtype=F32) / half)
    ang = jnp.arange(seq, dtype=F32)[:, None] * inv[None, :]
    cos = jnp.cos(ang)
    sin = jnp.sin(ang)
    n_heads = ATTN_WIDTH // HEAD_DIM
    cos_t = jnp.tile(jnp.concatenate([cos, cos], axis=1), (1, n_heads))
    sin_t = jnp.tile(jnp.concatenate([-sin, sin], axis=1), (1, n_heads))
    return cos_t, sin_t


def _layer(x, p, mix_norm, w_in, gmlp_ln_g, gmlp_ln_b, gmlp_ws, gmlp_bs, attn_out_norm, gmlp_out_norm,
           w_out, ffn_norm, router_w, router_b, w_gate_up, b_gate_up, w_down, b_down, ple_norm,
           w_ple_gate, w_ple_proj, ple_post_norm, out_norm):
    B, S, D = x.shape
    T = B * S
    A = ATTN_WIDTH
    n_tiles = T // TM
    n_blk = S // MOBA_BLOCK
    tiles_per_seq = S // TM
    xf = x.reshape(T, D)
    row2 = lambda v: v.reshape(1, -1)
    full = lambda shape: pl.BlockSpec(shape, lambda *_: (0,) * len(shape))

    cos_t, sin_t = _rope_tables(S)
    wk = w_in[:, A:2 * A].astype(BF16)
    wqvt = jnp.concatenate([w_in[:, :A], w_in[:, 2 * A:3 * A]], axis=1).T.astype(BF16)
    wuv = w_in[:, 3 * A:].astype(BF16)

    qt, k, vt, kmean, gm = pl.pallas_call(
        _proj_kernel,
        grid=(n_tiles,),
        in_specs=[pl.BlockSpec((TM, D), lambda i: (i, 0)), full((1, D)), full((D, A)), full((2 * A, D)),
                  full((D, 2 * GMLP_WIDTH)),
                  pl.BlockSpec((TM, A), lambda i: (i % tiles_per_seq, 0)),
                  pl.BlockSpec((TM, A), lambda i: (i % tiles_per_seq, 0)),
                  pl.BlockSpec((A, TM), lambda i: (0, i % tiles_per_seq)),
                  pl.BlockSpec((A, TM), lambda i: (0, i % tiles_per_seq)),
                  full((1, GMLP_WIDTH)), full((1, GMLP_WIDTH)),
                  full((GMLP_GROUPS, GMLP_CHUNK, GMLP_CHUNK)), full((GMLP_GROUPS, GMLP_CHUNK, 1)),
                  full((1, GMLP_WIDTH))],
        out_specs=[pl.BlockSpec((A, TM), lambda i: (0, i)), pl.BlockSpec((TM, A), lambda i: (i, 0)),
                   pl.BlockSpec((A, TM), lambda i: (0, i)),
                   pl.BlockSpec((1, TM // MOBA_BLOCK, A), lambda i: (i, 0, 0)),
                   pl.BlockSpec((TM, GMLP_WIDTH), lambda i: (i, 0))],
        out_shape=[jax.ShapeDtypeStruct((A, T), BF16), jax.ShapeDtypeStruct((T, A), BF16),
                   jax.ShapeDtypeStruct((A, T), BF16),
                   jax.ShapeDtypeStruct((n_tiles, TM // MOBA_BLOCK, A), F32),
                   jax.ShapeDtypeStruct((T, GMLP_WIDTH), BF16)],
        scratch_shapes=[pltpu.VMEM((TM, GMLP_WIDTH), F32)],
        compiler_params=_params(("parallel",)),
        name="proj",
    )(xf, row2(mix_norm), wk, wqvt, wuv, cos_t, sin_t, cos_t.T, sin_t.T, row2(gmlp_ln_g), row2(gmlp_ln_b), gmlp_ws,
      gmlp_bs.reshape(GMLP_GROUPS, GMLP_CHUNK, 1), row2(gmlp_out_norm))

    kmean = kmean.reshape(B, n_blk, A)
    n_pairs = A // LANES
    attn_t = pl.pallas_call(
        _attn_kernel,
        grid=(B, n_pairs),
        in_specs=[pl.BlockSpec((LANES, S), lambda b, h: (h, b)), pl.BlockSpec((S, LANES), lambda b, h: (b, h)),
                  pl.BlockSpec((LANES, S), lambda b, h: (h, b)),
                  pl.BlockSpec((1, n_blk, LANES), lambda b, h: (b, 0, h))],
        out_specs=pl.BlockSpec((n_blk, LANES, MOBA_BLOCK), lambda b, h: (b, h, 0)),
        out_shape=jax.ShapeDtypeStruct((T // MOBA_BLOCK, A, MOBA_BLOCK), F32),
        compiler_params=_params(("parallel", "parallel")),
        name="attn",
    )(qt, k, vt, kmean)

    x1, xs, lpos, gates, tab, tot = pl.pallas_call(
        _mix_kernel,
        grid=(n_tiles,),
        in_specs=[pl.BlockSpec((TM, D), lambda i: (i, 0)),
                  pl.BlockSpec((TM // MOBA_BLOCK, A, MOBA_BLOCK), lambda i: (i, 0, 0)),
                  pl.BlockSpec((TM, GMLP_WIDTH), lambda i: (i, 0)),
                  full((A, 1)), full((A, D)), full((GMLP_WIDTH, D)), full((1, D)),
                  full((N_EXPERTS, D)), full((N_EXPERTS, 1))],
        out_specs=[pl.BlockSpec((TM, D), lambda i: (i, 0)), pl.BlockSpec((SORT_ROWS, D), lambda i: (i, 0)),
                   pl.BlockSpec((TOP_K, TM), lambda i: (0, i)), pl.BlockSpec((TOP_K, TM), lambda i: (0, i)),
                   pl.BlockSpec((1, N_EXPERTS, LANES), lambda i: (i, 0, 0)), full((N_EXPERTS, LANES))],
        out_shape=[jax.ShapeDtypeStruct((T, D), F32), jax.ShapeDtypeStruct((n_tiles * SORT_ROWS, D), BF16),
                   jax.ShapeDtypeStruct((TOP_K, T), I32), jax.ShapeDtypeStruct((TOP_K, T), F32),
                   jax.ShapeDtypeStruct((n_tiles, N_EXPERTS, LANES), I32),
                   jax.ShapeDtypeStruct((N_EXPERTS, LANES), I32)],
        scratch_shapes=[pltpu.VMEM((N_EXPERTS, LANES), F32)],
        compiler_params=_params(("arbitrary",)),
        name="mix",
    )(xf, attn_t, gm, attn_out_norm.reshape(A, 1), w_out[:A].astype(BF16), w_out[A:].astype(BF16),
      row2(ffn_norm), router_w.T, router_b.reshape(N_EXPERTS, 1))

    n_blocks = -(-(T * TOP_K + n_tiles * N_EXPERTS * (CHUNK - 1) + N_EXPERTS * (RB - CHUNK)) // RB)
    seg16, off16, before16 = tab[:, :, 0], tab[:, :, 1], tab[:, :, 2]
    tot16 = tot[:, 0]
    blocks_e = (tot16 + CHUNKS_PER_BLOCK - 1) // CHUNKS_PER_BLOCK
    bend = jnp.cumsum(blocks_e)
    nused = bend[-1:].astype(I32)
    bidx = jnp.arange(n_blocks, dtype=I32)
    blk_e = jnp.minimum(jnp.sum(bend[None, :] <= bidx[:, None], axis=1), N_EXPERTS - 1).astype(I32)
    sel = blk_e[:, None] == jnp.arange(N_EXPERTS, dtype=I32)[None, :]
    pick = lambda v: jnp.sum(jnp.where(sel, v[None, :], 0), axis=1)
    chunk_pos = ((bidx - pick(bend - blocks_e)) * CHUNKS_PER_BLOCK)[:, None] \
        + jnp.arange(CHUNKS_PER_BLOCK, dtype=I32)[None, :]
    valid = (chunk_pos < pick(tot16)[:, None]) & (bidx < nused)[:, None]
    seg_end = jnp.sum(jnp.where(sel[:, None, :], (before16 + seg16)[None], 0), axis=2)
    shift = jnp.sum(jnp.where(sel[:, None, :], (off16 - before16)[None], 0), axis=2)
    tile = jnp.minimum(jnp.sum(seg_end[:, None, :] <= chunk_pos[:, :, None], axis=2), n_tiles - 1)
    in_tile = tile[:, :, None] == jnp.arange(n_tiles, dtype=I32)[None, None, :]
    where16 = tile * CHUNKS_PER_TILE + jnp.sum(jnp.where(in_tile, shift[:, None, :], 0), axis=2) + chunk_pos
    dump16 = (n_tiles * CHUNKS_PER_TILE + (bidx % 2)[:, None] * CHUNKS_PER_BLOCK
              + jnp.arange(CHUNKS_PER_BLOCK, dtype=I32)[None, :])
    tabs = jnp.concatenate([jnp.where(valid, where16, 0), jnp.where(valid, where16, dump16)],
                           axis=1).astype(I32).reshape(n_blocks, 1, 2 * CHUNKS_PER_BLOCK)

    wslot = pick((jnp.cumsum(blocks_e > 0) - 1) % 2).astype(I32)
    next_start = pick(bend)
    next_e = jnp.where(next_start < nused,
                       jnp.minimum(jnp.sum(bend[None, :] <= next_start[:, None], axis=1), N_EXPERTS - 1),
                       -1).astype(I32)

    last = lambda i, nu: jnp.minimum(i, nu[0] - 1)
    smem_tab = lambda step: pl.BlockSpec((1, 1, 2 * CHUNKS_PER_BLOCK),
                                         lambda i, be, nu, ws, ne: (jnp.minimum(i + step, n_blocks - 1), 0, 0),
                                         memory_space=pltpu.SMEM)
    y = pl.pallas_call(
        _expert_kernel,
        grid_spec=pltpu.PrefetchScalarGridSpec(
            num_scalar_prefetch=4, grid=(n_blocks,),
            in_specs=[smem_tab(0), smem_tab(1), pl.BlockSpec(memory_space=pl.ANY),
                      pl.BlockSpec(memory_space=pl.ANY),
                      pl.BlockSpec((1, 1, 2 * D_EXPERT), lambda i, be, nu, ws, ne: (be[last(i, nu)], 0, 0)),
                      pl.BlockSpec(memory_space=pl.ANY),
                      pl.BlockSpec((1, 1, D), lambda i, be, nu, ws, ne: (be[last(i, nu)], 0, 0))],
            out_specs=pl.BlockSpec(memory_space=pl.ANY),
            scratch_shapes=[pltpu.VMEM((2, RB, D), BF16), pltpu.VMEM((2, RB, D), BF16),
                            pltpu.VMEM((SORT_ROWS - PAIRS, D), BF16),
                            pltpu.VMEM((2, D, 2 * D_EXPERT), F32), pltpu.VMEM((2, D_EXPERT, D), F32),
                            pltpu.VMEM((D, 2 * D_EXPERT), BF16), pltpu.VMEM((D_EXPERT, D), BF16),
                            pltpu.SemaphoreType.DMA((2,)), pltpu.SemaphoreType.DMA((2,)),
                            pltpu.SemaphoreType.DMA(()), pltpu.SemaphoreType.DMA((2,))]),
        out_shape=jax.ShapeDtypeStruct((n_tiles * SORT_ROWS + DUMP_ROWS, D), BF16),
        compiler_params=pltpu.CompilerParams(dimension_semantics=("arbitrary",),
                                             vmem_limit_bytes=EXPERT_VMEM_LIMIT),
        name="experts",
    )(blk_e, nused, wslot, next_e, tabs, tabs, xs, w_gate_up, b_gate_up.reshape(N_EXPERTS, 1, -1), w_down,
      b_down.reshape(N_EXPERTS, 1, -1))

    out = pl.pallas_call(
        _final_kernel,
        grid=(n_tiles,),
        in_specs=[pl.BlockSpec((SORT_ROWS, D), lambda i: (i, 0)),
                  pl.BlockSpec((TM, D), lambda i: (i, 0)), pl.BlockSpec((TM, TOP_K), lambda i: (i, 0)),
                  pl.BlockSpec((TM, TOP_K), lambda i: (i, 0)),
                  pl.BlockSpec((TM, PLE_DIM), lambda i: (i, 0)), full((1, D)), full((D, D)), full((PLE_DIM, D)),
                  full((1, D)), full((1, D))],
        out_specs=pl.BlockSpec((TM, D), lambda i: (i, 0)),
        out_shape=jax.ShapeDtypeStruct((T, D), F32),
        compiler_params=_params(("parallel",)),
        name="final",
    )(y, x1, lpos.T, gates.T, p.reshape(T, PLE_DIM), row2(ple_norm), w_ple_gate.astype(BF16),
      w_ple_proj.astype(BF16), row2(ple_post_norm), row2(out_norm))
    return out.reshape(B, S, D)


def kernel(x, p, mix_norm, w_in, gmlp_ln_g, gmlp_ln_b, gmlp_ws, gmlp_bs, attn_out_norm, gmlp_out_norm, w_out, ffn_norm, router_w, router_b, w_gate_up, b_gate_up, w_down, b_down, ple_norm, w_ple_gate, w_ple_proj, ple_post_norm, final_norm):
    depth = p.shape[0]
    assert depth == 1, "the final rmsnorm is fused into the (single) layer"
    return _layer(x, p[0], mix_norm[0], w_in[0], gmlp_ln_g[0], gmlp_ln_b[0], gmlp_ws[0], gmlp_bs[0],
                  attn_out_norm[0], gmlp_out_norm[0], w_out[0], ffn_norm[0], router_w[0], router_b[0],
                  w_gate_up[0], b_gate_up[0], w_down[0], b_down[0], ple_norm[0], w_ple_gate[0],
                  w_ple_proj[0], ple_post_norm[0], final_norm)
```
